```python
import jax, jax.numpy as jnp
from jax import lax
import numpy as np

D_MODEL = 1024
BATCH = 1
SEQ = 16384
DEPTH = 1

CONV_WIDTH = D_MODEL
CONV_KERNEL = 31
N_RET_HEADS = 8
RET_QK_DIM = D_MODEL // N_RET_HEADS
RET_V_DIM = 2 * RET_QK_DIM
RET_QK_WIDTH = N_RET_HEADS * RET_QK_DIM
RET_V_WIDTH = N_RET_HEADS * RET_V_DIM
RET_CHUNK = 128
ROPE_BASE = 10000.0
D_FF = 2816
FFN_CONV_KERNEL = 3
LN_EPS = 1e-5
DEEPNORM_ALPHA = (2.0 * DEPTH) ** 0.25
DEEPNORM_BETA = (8.0 * DEPTH) ** -0.25
N_MOD = 6
IN_SIZES = (RET_QK_WIDTH, RET_QK_WIDTH, RET_V_WIDTH, RET_V_WIDTH,
            CONV_WIDTH, CONV_WIDTH, D_MODEL, D_MODEL)
IN_WIDTH = sum(IN_SIZES)
IN_SPLITS = tuple(int(s) for s in np.cumsum(IN_SIZES)[:-1])

kernel_name = "hybrid_conformer_retention_deepnorm_block"


def layer_norm(x, g=None, b=None):
    xf = x.astype(jnp.float32)
    mu = jnp.mean(xf, axis=-1, keepdims=True)
    var = jnp.mean(jnp.square(xf - mu), axis=-1, keepdims=True)
    y = ((xf - mu) * lax.rsqrt(var + LN_EPS)).astype(x.dtype)
    if g is not None:
        y = y * g + b
    return y


def causal_depthwise_conv(x, w, b):
    k_width, ch = w.shape
    y = lax.conv_general_dilated(
        x, w[:, None, :].astype(x.dtype), window_strides=(1,),
        padding=[(k_width - 1, 0)], dimension_numbers=("NWC", "WIO", "NWC"),
        feature_group_count=ch)
    return y + b


def rotary(x, positions):
    half = x.shape[-1] // 2
    inv_freq = ROPE_BASE ** (-jnp.arange(half, dtype=jnp.float32) / half)
    ang = positions.astype(jnp.float32)[..., None] * inv_freq
    cos = jnp.cos(ang)[:, :, None, :]
    sin = jnp.sin(ang)[:, :, None, :]
    x1, x2 = x[..., :half], x[..., half:]
    return jnp.concatenate([x1 * cos - x2 * sin, x2 * cos + x1 * sin], axis=-1)


def chunkwise_retention(q, k, v):
    bsz, seq, heads, dk = q.shape
    dv = v.shape[-1]
    n_chunks = seq // RET_CHUNK
    log_gamma = jnp.log(1.0 - 2.0 ** (-5.0 - jnp.arange(heads, dtype=jnp.float32)))
    q = q.reshape(bsz, n_chunks, RET_CHUNK, heads, dk)
    k = k.reshape(bsz, n_chunks, RET_CHUNK, heads, dk)
    v = v.reshape(bsz, n_chunks, RET_CHUNK, heads, dv)
    idx = jnp.arange(RET_CHUNK, dtype=jnp.float32)
    rel = idx[:, None] - idx[None, :]
    decay = jnp.where(rel[None] >= 0,
                      jnp.exp(log_gamma[:, None, None] * jnp.maximum(rel, 0.0)[None]),
                      0.0)
    scores = jnp.einsum("bnihd,bnjhd->bnhij", q, k) * decay
    inner = jnp.einsum("bnhij,bnjhe->bnihe", scores, v)
    zeta = jnp.exp(log_gamma[:, None] * (RET_CHUNK - 1.0 - idx)[None])
    kv = jnp.einsum("bnjhd,bnjhe->bnhde", k * zeta.T[None, None, :, :, None], v)
    chunk_decay = jnp.exp(log_gamma * RET_CHUNK)[None, :, None, None]

    def step(state, kv_c):
        return state * chunk_decay + kv_c, state

    _, r_prev = lax.scan(step, jnp.zeros((bsz, heads, dk, dv), jnp.float32),
                         jnp.moveaxis(kv, 1, 0))
    r_prev = jnp.moveaxis(r_prev, 0, 1)
    xi = jnp.exp(log_gamma[:, None] * (idx + 1.0)[None])
    cross = jnp.einsum("bnihd,bnhde->bnihe", q * xi.T[None, None, :, :, None], r_prev)
    return (inner + cross).reshape(bsz, seq, heads, dv)


def token_mixer(h, positions, w_in, b_in, conv_dw_w, conv_dw_b, conv_ln_g, conv_ln_b,
                w_conv_out, ret_gn_g, ret_gn_b, w_ret_out, w_out):
    bsz, seq, _ = h.shape
    proj = jnp.einsum("bsd,de->bse", h, w_in) + b_in
    q, k, v, g_ret, c_val, c_gate, gate_a, gate_b = jnp.split(proj, IN_SPLITS, axis=-1)

    a = c_val * jax.nn.sigmoid(c_gate)
    a = causal_depthwise_conv(a, conv_dw_w, conv_dw_b)
    a = jax.nn.silu(layer_norm(a, conv_ln_g, conv_ln_b))
    y_a = jnp.einsum("bsc,cd->bsd", a, w_conv_out)

    qf = rotary(q.reshape(bsz, seq, N_RET_HEADS, RET_QK_DIM).astype(jnp.float32), positions)
    kf = rotary(k.reshape(bsz, seq, N_RET_HEADS, RET_QK_DIM).astype(jnp.float32), positions)
    kf = kf * (RET_QK_DIM ** -0.5)
    vf = v.reshape(bsz, seq, N_RET_HEADS, RET_V_DIM).astype(jnp.float32)
    r = chunkwise_retention(qf, kf, vf)
    mu = jnp.mean(r, axis=-1, keepdims=True)
    var = jnp.mean(jnp.square(r - mu), axis=-1, keepdims=True)
    r = ((r - mu) * lax.rsqrt(var + LN_EPS)).reshape(bsz, seq, RET_V_WIDTH).astype(h.dtype)
    r = (r * ret_gn_g + ret_gn_b) * jax.nn.silu(g_ret)
    y_b = jnp.einsum("bse,ed->bsd", r, w_ret_out)

    m = jax.nn.sigmoid(gate_a) * y_a + jax.nn.sigmoid(gate_b) * y_b
    return jnp.einsum("bsd,de->bse", m, w_out)


def channel_mixer(h, w_up, ffn_dw_w, ffn_dw_b, w_down):
    u = jnp.einsum("bsd,df->bsf", h, w_up)
    u = causal_depthwise_conv(u, ffn_dw_w, ffn_dw_b)
    val, gate = jnp.split(u, 2, axis=-1)
    return jnp.einsum("bsf,fd->bsd", val * jax.nn.silu(gate), w_down)


def setup_inputs(seed: int = 0) -> dict:
    key = jax.random.key(seed)
    ks = jax.random.split(key, 32)
    f32 = jnp.float32

    def nrm(k, shape, scale):
        return jax.random.normal(k, shape, f32) * scale

    L, D = DEPTH, D_MODEL
    x = jax.random.normal(ks[0], (BATCH, SEQ, D), f32)
    c = jax.random.normal(ks[1], (BATCH, D), f32)
    offset = jax.random.randint(ks[2], (BATCH, 1), 0, 1024, dtype=jnp.int32)
    positions = offset + jnp.arange(SEQ, dtype=jnp.int32)[None, :]
    return {
        "x": x,
        "c": c,
        "positions": positions,
        "w_ada": nrm(ks[3], (L, D, N_MOD * D), 0.5 * D ** -0.5),
        "b_ada": nrm(ks[4], (L, N_MOD * D), 0.02),
        "w_in": nrm(ks[5], (L, D, IN_WIDTH), D ** -0.5),
        "b_in": nrm(ks[6], (L, IN_WIDTH), 0.02),
        "conv_dw_w": nrm(ks[7], (L, CONV_KERNEL, CONV_WIDTH), CONV_KERNEL ** -0.5),
        "conv_dw_b": nrm(ks[8], (L, CONV_WIDTH), 0.02),
        "conv_ln_g": 1.0 + nrm(ks[9], (L, CONV_WIDTH), 0.02),
        "conv_ln_b": nrm(ks[10], (L, CONV_WIDTH), 0.02),
        "w_conv_out": nrm(ks[11], (L, CONV_WIDTH, D), CONV_WIDTH ** -0.5),
        "ret_gn_g": 1.0 + nrm(ks[12], (L, RET_V_WIDTH), 0.02),
        "ret_gn_b": nrm(ks[13], (L, RET_V_WIDTH), 0.02),
        "w_ret_out": nrm(ks[14], (L, RET_V_WIDTH, D), RET_V_WIDTH ** -0.5),
        "w_out": nrm(ks[15], (L, D, D), DEEPNORM_BETA * D ** -0.5),
        "ln1_g": 1.0 + nrm(ks[16], (L, D), 0.02),
        "ln1_b": nrm(ks[17], (L, D), 0.02),
        "w_up": nrm(ks[18], (L, D, 2 * D_FF), D ** -0.5),
        "ffn_dw_w": nrm(ks[19], (L, FFN_CONV_KERNEL, 2 * D_FF), FFN_CONV_KERNEL ** -0.5),
        "ffn_dw_b": nrm(ks[20], (L, 2 * D_FF), 0.02),
        "w_down": nrm(ks[21], (L, D_FF, D), DEEPNORM_BETA * D_FF ** -0.5),
        "ln2_g": 1.0 + nrm(ks[22], (L, D), 0.02),
        "ln2_b": nrm(ks[23], (L, D), 0.02),
    }


def reference(x, c, positions, w_ada, b_ada, w_in, b_in, conv_dw_w, conv_dw_b, conv_ln_g,
              conv_ln_b, w_conv_out, ret_gn_g, ret_gn_b, w_ret_out, w_out, ln1_g, ln1_b,
              w_up, ffn_dw_w, ffn_dw_b, w_down, ln2_g, ln2_b):
    for l in range(DEPTH):
        mod = jnp.einsum("bd,de->be", jax.nn.silu(c), w_ada[l]) + b_ada[l]
        shift1, scale1, gate1, shift2, scale2, gate2 = jnp.split(mod[:, None, :], N_MOD, axis=-1)

        h = layer_norm(x) * (1.0 + scale1) + shift1
        t = token_mixer(h, positions, w_in[l], b_in[l], conv_dw_w[l], conv_dw_b[l],
                        conv_ln_g[l], conv_ln_b[l], w_conv_out[l], ret_gn_g[l], ret_gn_b[l],
                        w_ret_out[l], w_out[l])
        x = layer_norm(DEEPNORM_ALPHA * x + gate1 * t, ln1_g[l], ln1_b[l])

        h = layer_norm(x) * (1.0 + scale2) + shift2
        f = channel_mixer(h, w_up[l], ffn_dw_w[l], ffn_dw_b[l], w_down[l])
        x = layer_norm(DEEPNORM_ALPHA * x + gate2 * f, ln2_g[l], ln2_b[l])
    return x
```

```python
import functools

import jax
import jax.numpy as jnp
import numpy as np
from jax import lax
from jax.experimental import pallas as pl
from jax.experimental.pallas import tpu as pltpu

F32 = jnp.float32
BF16 = jnp.bfloat16

DEPTH = 1
N_HEADS = 8
RET_CHUNK = 128
CONV_KERNEL = 31
FFN_CONV_KERNEL = 3
ROPE_BASE = 10000.0
LN_EPS = 1e-5
DEEPNORM_ALPHA = (2.0 * DEPTH) ** 0.25
N_MOD = 6

V7X_LANES = 128
V7X_SUBLANES = 8
V7X_VMEM_BYTES = 64 * 1024 * 1024

SEQ_TILE = 512
CONV_HIST = 32
FFN_HIST = V7X_SUBLANES
CONV_ROWS = 128
FFN_COLS = 256


def _ln(x):
    mu = jnp.mean(x, axis=-1, keepdims=True)
    xc = x - mu
    var = jnp.mean(xc * xc, axis=-1, keepdims=True)
    return xc * lax.rsqrt(var + LN_EPS)


def _sigmoid(x):
    return jax.nn.sigmoid(x)


def _resident(shape):
    nd = len(shape)
    return pl.BlockSpec(shape, lambda i: (0,) * nd, pipeline_mode=pl.Buffered(1))


def _rows(tile, width):
    return pl.BlockSpec((tile, width), lambda i: (i, 0))


def _params(vmem_bytes):
    return pltpu.CompilerParams(dimension_semantics=("arbitrary",), vmem_limit_bytes=vmem_bytes)


def _mod_kernel(c_ref, w_ref, b_ref, o_ref):
    c = c_ref[...]
    s = c * _sigmoid(c)
    o_ref[...] = jnp.dot(s, w_ref[...], preferred_element_type=F32) + b_ref[...]


def _adaln_mod(c, w_ada, b_ada):
    d = c.shape[-1]
    c8 = jnp.broadcast_to(c, (V7X_SUBLANES, d))
    out = pl.pallas_call(
        _mod_kernel,
        grid=(N_MOD,),
        in_specs=[
            pl.BlockSpec((V7X_SUBLANES, d), lambda i: (0, 0)),
            pl.BlockSpec((d, d), lambda i: (0, i)),
            pl.BlockSpec((1, d), lambda i: (0, i)),
        ],
        out_specs=pl.BlockSpec((V7X_SUBLANES, d), lambda i: (0, i)),
        out_shape=jax.ShapeDtypeStruct((V7X_SUBLANES, N_MOD * d), F32),
        compiler_params=pltpu.CompilerParams(dimension_semantics=("arbitrary",)),
        name="adaln_mod",
    )(c8, w_ada, b_ada.reshape(1, -1))
    return out[0].reshape(N_MOD, d)


def _ret_kernel(x_ref, pos_ref, invf_ref, mod_ref, w_ref, b_ref, gng_ref, gnb_ref, wro_ref,
                decay_ref, xi_ref, zeta_ref, o_ref,
                state_ref, q_ref, qx_ref, k_ref, kz_ref, v_ref, sg_ref, ract_ref,
                *, tile, d_model, dk, dv, chunk_decay):
    n_heads = d_model // dk
    qk_w = n_heads * dk
    v_w = n_heads * dv
    o_q, o_k, o_v, o_g, o_gb = 0, qk_w, 2 * qk_w, 2 * qk_w + v_w, 2 * qk_w + 2 * v_w
    n_chunks = tile // RET_CHUNK

    @pl.when(pl.program_id(0) == 0)
    def _():
        state_ref[...] = jnp.zeros_like(state_ref)

    x = x_ref[...]
    h = _ln(x) * (1.0 + mod_ref[1:2, :]) + mod_ref[0:1, :]
    hb = h.astype(BF16)

    def proj(lo, hi):
        return jnp.dot(hb, w_ref[:, lo:hi], preferred_element_type=F32) + b_ref[:, lo:hi]

    ang = pos_ref[...] * invf_ref[...]
    cs = jnp.cos(ang)
    sn = jnp.sin(ang)
    lane = lax.broadcasted_iota(jnp.int32, ang.shape, 1)
    sn = jnp.where(lane < dk // 2, -sn, sn)
    k_scale = float(dk) ** -0.5

    def rope_heads(y, cos_t, sin_t):
        for hh in range(n_heads):
            yh = y[:, hh * dk:(hh + 1) * dk]
            yield hh, yh * cos_t + pltpu.roll(yh, dk // 2, 1) * sin_t

    def tiled(tab_ref, hh):
        return jnp.concatenate([tab_ref[:, hh * dk:(hh + 1) * dk]] * n_chunks, axis=0)

    for hh, r in rope_heads(proj(o_q, o_k), cs, sn):
        q_ref[:, hh * dk:(hh + 1) * dk] = r.astype(BF16)
        qx_ref[:, hh * dk:(hh + 1) * dk] = (r * tiled(xi_ref, hh)).astype(BF16)
    for hh, r in rope_heads(proj(o_k, o_v), cs * k_scale, sn * k_scale):
        k_ref[:, hh * dk:(hh + 1) * dk] = r.astype(BF16)
        kz_ref[:, hh * dk:(hh + 1) * dk] = (r * tiled(zeta_ref, hh)).astype(BF16)
    v_ref[...] = proj(o_v, o_g).astype(BF16)
    g = proj(o_g, o_gb)
    sg_ref[...] = g * _sigmoid(g)

    for ci in range(n_chunks):
        rows = slice(ci * RET_CHUNK, (ci + 1) * RET_CHUNK)
        for hh in range(n_heads):
            qs = slice(hh * dk, (hh + 1) * dk)
            vs = slice(hh * dv, (hh + 1) * dv)
            vh = v_ref[rows, vs]
            s = lax.dot_general(q_ref[rows, qs], k_ref[rows, qs], (((1,), (1,)), ((), ())),
                                preferred_element_type=F32)
            s = (s * decay_ref[hh]).astype(BF16)
            st = state_ref[hh]
            lhs = jnp.concatenate([s, qx_ref[rows, qs]], axis=1)
            rhs = jnp.concatenate([vh, st.astype(BF16)], axis=0)
            o = jnp.dot(lhs, rhs, preferred_element_type=F32)
            kv = lax.dot_general(kz_ref[rows, qs], vh, (((0,), (0,)), ((), ())),
                                 preferred_element_type=F32)
            state_ref[hh] = st * chunk_decay[hh] + kv
            rn = _ln(o)
            r = (rn * gng_ref[:, vs] + gnb_ref[:, vs]) * sg_ref[rows, vs]
            ract_ref[rows, vs] = r.astype(BF16)

    yb = jnp.dot(ract_ref[...], wro_ref[...], preferred_element_type=F32)
    o_ref[...] = _sigmoid(proj(o_gb, o_gb + d_model)) * yb


def _retention_branch(x2, pos, mod, w_a, b_a, gn_g, gn_b, w_ro, tile):
    s_len, d = x2.shape
    dk = d // N_HEADS
    dv = w_ro.shape[0] // N_HEADS
    log_gamma = np.log(1.0 - 2.0 ** (-5.0 - np.arange(N_HEADS, dtype=np.float64)))
    idx = np.arange(RET_CHUNK, dtype=np.float64)
    rel = idx[:, None] - idx[None, :]
    decay = np.where(rel[None] >= 0, np.exp(log_gamma[:, None, None] * np.maximum(rel, 0.0)[None]), 0.0)
    xi = np.repeat(np.exp(log_gamma[None, :] * (idx[:, None] + 1.0)), dk, axis=1)
    zeta = np.repeat(np.exp(log_gamma[None, :] * (RET_CHUNK - 1.0 - idx[:, None])), dk, axis=1)
    chunk_decay = tuple(float(v) for v in np.exp(log_gamma * RET_CHUNK))
    half = dk // 2
    inv_freq = ROPE_BASE ** (-jnp.arange(half, dtype=F32) / half)
    inv_freq = jnp.concatenate([inv_freq, inv_freq]).reshape(1, dk)

    in_w = w_a.shape[1]
    kern = functools.partial(_ret_kernel, tile=tile, d_model=d, dk=dk, dv=dv, chunk_decay=chunk_decay)
    return pl.pallas_call(
        kern,
        grid=(s_len // tile,),
        in_specs=[
            _rows(tile, d),
            _rows(tile, 1),
            _resident((1, dk)),
            _resident((N_MOD, d)),
            _resident((d, in_w)),
            _resident((1, in_w)),
            _resident((1, N_HEADS * dv)),
            _resident((1, N_HEADS * dv)),
            _resident((N_HEADS * dv, d)),
            _resident((N_HEADS, RET_CHUNK, RET_CHUNK)),
            _resident((RET_CHUNK, d)),
            _resident((RET_CHUNK, d)),
        ],
        out_specs=_rows(tile, d),
        out_shape=jax.ShapeDtypeStruct((s_len, d), F32),
        scratch_shapes=[
            pltpu.VMEM((N_HEADS, dk, dv), F32),
            pltpu.VMEM((tile, d), BF16),
            pltpu.VMEM((tile, d), BF16),
            pltpu.VMEM((tile, d), BF16),
            pltpu.VMEM((tile, d), BF16),
            pltpu.VMEM((tile, N_HEADS * dv), BF16),
            pltpu.VMEM((tile, N_HEADS * dv), F32),
            pltpu.VMEM((tile, N_HEADS * dv), BF16),
        ],
        compiler_params=_params(56 * 1024 * 1024),
        name="retention_branch",
    )(x2, pos, inv_freq, mod, w_a, b_a, gn_g, gn_b, w_ro,
      jnp.asarray(decay, F32), jnp.asarray(xi, F32), jnp.asarray(zeta, F32))


def _conv_kernel(x_ref, zb_ref, mod_ref, w_ref, b_ref, cw_ref, cb_ref, clg_ref, clb_ref,
                 wco_ref, wout_ref, l1g_ref, l1b_ref, o_ref, abuf_ref, *, tile, d_model):
    n_lane_chunks = d_model // V7X_LANES

    @pl.when(pl.program_id(0) == 0)
    def _():
        abuf_ref[:, 0:CONV_HIST, :] = jnp.zeros((n_lane_chunks, CONV_HIST, V7X_LANES), F32)

    x = x_ref[...]
    h = _ln(x) * (1.0 + mod_ref[1:2, :]) + mod_ref[0:1, :]
    hb = h.astype(BF16)

    def proj(lo, hi):
        return jnp.dot(hb, w_ref[:, lo:hi], preferred_element_type=F32) + b_ref[:, lo:hi]

    a = proj(0, d_model) * _sigmoid(proj(d_model, 2 * d_model))
    for j in range(n_lane_chunks):
        abuf_ref[j, CONV_HIST:CONV_HIST + tile, :] = a[:, j * V7X_LANES:(j + 1) * V7X_LANES]

    first = CONV_HIST - (CONV_KERNEL - 1)
    cols = []
    for j in range(n_lane_chunks):
        ls = slice(j * V7X_LANES, (j + 1) * V7X_LANES)
        blocks = []
        for rb in range(tile // CONV_ROWS):
            acc = jnp.broadcast_to(cb_ref[:, ls], (CONV_ROWS, V7X_LANES))
            for k in range(CONV_KERNEL):
                r0 = rb * CONV_ROWS + first + k
                acc = acc + cw_ref[k:k + 1, ls] * abuf_ref[j, r0:r0 + CONV_ROWS, :]
            blocks.append(acc)
        cols.append(jnp.concatenate(blocks, axis=0))
        abuf_ref[j, 0:CONV_HIST, :] = abuf_ref[j, tile:tile + CONV_HIST, :]
    ac = jnp.concatenate(cols, axis=1)

    an = _ln(ac) * clg_ref[...] + clb_ref[...]
    sa = an * _sigmoid(an)
    ya = jnp.dot(sa.astype(BF16), wco_ref[...], preferred_element_type=F32)
    ga = _sigmoid(proj(2 * d_model, 3 * d_model))
    m = ga * ya + zb_ref[...]
    t = jnp.dot(m.astype(BF16), wout_ref[...], preferred_element_type=F32)
    z = DEEPNORM_ALPHA * x + mod_ref[2:3, :] * t
    o_ref[...] = _ln(z) * l1g_ref[...] + l1b_ref[...]


def _conv_merge(x2, zb, mod, w_b, b_b, cw, cb, clg, clb, w_co, w_out, l1g, l1b, tile):
    s_len, d = x2.shape
    kern = functools.partial(_conv_kernel, tile=tile, d_model=d)
    return pl.pallas_call(
        kern,
        grid=(s_len // tile,),
        in_specs=[
            _rows(tile, d),
            _rows(tile, d),
            _resident((N_MOD, d)),
            _resident(w_b.shape),
            _resident(b_b.shape),
            _resident(cw.shape),
            _resident((1, d)),
            _resident((1, d)),
            _resident((1, d)),
            _resident((d, d)),
            _resident((d, d)),
            _resident((1, d)),
            _resident((1, d)),
        ],
        out_specs=_rows(tile, d),
        out_shape=jax.ShapeDtypeStruct((s_len, d), F32),
        scratch_shapes=[pltpu.VMEM((d // V7X_LANES, CONV_HIST + tile, V7X_LANES), F32)],
        compiler_params=_params(48 * 1024 * 1024),
        name="conv_merge",
    )(x2, zb, mod, w_b, b_b, cw, cb, clg, clb, w_co, w_out, l1g, l1b)


def _ffn_kernel(x_ref, mod_ref, wup_ref, fw_ref, fb_ref, wdn_ref, l2g_ref, l2b_ref, o_ref,
                ubuf_ref, act_ref, *, tile, d_ff):
    n_lane_chunks = 2 * d_ff // V7X_LANES

    @pl.when(pl.program_id(0) == 0)
    def _():
        ubuf_ref[:, 0:FFN_HIST, :] = jnp.zeros((n_lane_chunks, FFN_HIST, V7X_LANES), F32)

    x = x_ref[...]
    h = _ln(x) * (1.0 + mod_ref[4:5, :]) + mod_ref[3:4, :]
    hb = h.astype(BF16)

    def conv_cols(base):
        u = jnp.dot(hb, wup_ref[:, base:base + FFN_COLS], preferred_element_type=F32)
        outs = []
        for jj in range(FFN_COLS // V7X_LANES):
            j = base // V7X_LANES + jj
            ls = slice(j * V7X_LANES, (j + 1) * V7X_LANES)
            uj = u[:, jj * V7X_LANES:(jj + 1) * V7X_LANES]
            ubuf_ref[j, FFN_HIST:FFN_HIST + tile, :] = uj
            y = fb_ref[:, ls] + fw_ref[2:3, ls] * uj
            for k in range(FFN_CONV_KERNEL - 1):
                r0 = FFN_HIST - (FFN_CONV_KERNEL - 1) + k
                y = y + fw_ref[k:k + 1, ls] * ubuf_ref[j, r0:r0 + tile, :]
            ubuf_ref[j, 0:FFN_HIST, :] = ubuf_ref[j, tile:tile + FFN_HIST, :]
            outs.append(y)
        return jnp.concatenate(outs, axis=1)

    for gi in range(d_ff // FFN_COLS):
        val = conv_cols(gi * FFN_COLS)
        gate = conv_cols(d_ff + gi * FFN_COLS)
        act_ref[:, gi * FFN_COLS:(gi + 1) * FFN_COLS] = (val * (gate * _sigmoid(gate))).astype(BF16)

    f = jnp.dot(act_ref[...], wdn_ref[...], preferred_element_type=F32)
    z = DEEPNORM_ALPHA * x + mod_ref[5:6, :] * f
    o_ref[...] = _ln(z) * l2g_ref[...] + l2b_ref[...]


def _channel_mixer(x2, mod, w_up, fw, fb, w_dn, l2g, l2b, tile):
    s_len, d = x2.shape
    d_ff = w_dn.shape[0]
    kern = functools.partial(_ffn_kernel, tile=tile, d_ff=d_ff)
    return pl.pallas_call(
        kern,
        grid=(s_len // tile,),
        in_specs=[
            _rows(tile, d),
            _resident((N_MOD, d)),
            _resident(w_up.shape),
            _resident(fw.shape),
            _resident((1, 2 * d_ff)),
            _resident(w_dn.shape),
            _resident((1, d)),
            _resident((1, d)),
        ],
        out_specs=_rows(tile, d),
        out_shape=jax.ShapeDtypeStruct((s_len, d), F32),
        scratch_shapes=[
            pltpu.VMEM((2 * d_ff // V7X_LANES, FFN_HIST + tile, V7X_LANES), F32),
            pltpu.VMEM((tile, d_ff), BF16),
        ],
        compiler_params=_params(56 * 1024 * 1024),
        name="channel_mixer",
    )(x2, mod, w_up, fw, fb, w_dn, l2g, l2b)


def kernel(x, c, positions, w_ada, b_ada, w_in, b_in, conv_dw_w, conv_dw_b, conv_ln_g, conv_ln_b,
           w_conv_out, ret_gn_g, ret_gn_b, w_ret_out, w_out, ln1_g, ln1_b, w_up, ffn_dw_w, ffn_dw_b,
           w_down, ln2_g, ln2_b):
    bsz, s_len, d = x.shape
    assert bsz == 1 and s_len % SEQ_TILE == 0 and d % V7X_LANES == 0
    depth = w_ada.shape[0]
    ret_v_w = w_ret_out.shape[1]
    qk_w = d
    split_conv = 2 * qk_w + 2 * ret_v_w
    split_gb = split_conv + 3 * d

    row = lambda v: v.reshape(1, -1)
    x2 = x.reshape(s_len, d)
    pos = positions.reshape(s_len, 1).astype(F32)
    for l in range(depth):
        mod = _adaln_mod(c, w_ada[l], b_ada[l])
        w_a = jnp.concatenate([w_in[l][:, :split_conv], w_in[l][:, split_gb:]], axis=1).astype(BF16)
        b_a = jnp.concatenate([b_in[l][:split_conv], b_in[l][split_gb:]]).reshape(1, -1)
        w_b = w_in[l][:, split_conv:split_gb].astype(BF16)
        b_b = b_in[l][split_conv:split_gb].reshape(1, -1)
        zb = _retention_branch(x2, pos, mod, w_a, b_a, row(ret_gn_g[l]), row(ret_gn_b[l]),
                               w_ret_out[l].astype(BF16), SEQ_TILE)
        x2 = _conv_merge(x2, zb, mod, w_b, b_b, conv_dw_w[l], row(conv_dw_b[l]), row(conv_ln_g[l]),
                         row(conv_ln_b[l]), w_conv_out[l].astype(BF16), w_out[l].astype(BF16),
                         row(ln1_g[l]), row(ln1_b[l]), SEQ_TILE)
        x2 = _channel_mixer(x2, mod, w_up[l].astype(BF16), ffn_dw_w[l], row(ffn_dw_b[l]),
                            w_down[l].astype(BF16), row(ln2_g[l]), row(ln2_b[l]), SEQ_TILE)
    return x2.reshape(bsz, s_len, d)
```

```python
import functools

import jax
import jax.numpy as jnp
import numpy as np
from jax import lax
from jax.experimental import pallas as pl
from jax.experimental.pallas import tpu as pltpu

F32 = jnp.float32
BF16 = jnp.bfloat16

DEPTH = 1
N_HEADS = 8
RET_CHUNK = 128
CONV_KERNEL = 31
FFN_CONV_KERNEL = 3
ROPE_BASE = 10000.0
LN_EPS = 1e-5
DEEPNORM_ALPHA = (2.0 * DEPTH) ** 0.25
N_MOD = 6

V7X_LANES = 128
V7X_SUBLANES = 8

SEQ_TILE = 512
CONV_HIST = 32
FFN_HIST = V7X_SUBLANES
CONV_ROWS = 128
FFN_COLS = 256
VMEM_LIMIT_BYTES = 60 * 1024 * 1024


def _ln(x):
    mu = jnp.mean(x, axis=-1, keepdims=True)
    xc = x - mu
    var = jnp.mean(xc * xc, axis=-1, keepdims=True)
    return xc * lax.rsqrt(var + LN_EPS)


def _sigmoid(x):
    return jax.nn.sigmoid(x)


def _resident(shape):
    nd = len(shape)
    return pl.BlockSpec(shape, lambda i: (0,) * nd, pipeline_mode=pl.Buffered(1))


def _rows(tile, width):
    return pl.BlockSpec((tile, width), lambda i: (i, 0))


def _params():
    return pltpu.CompilerParams(dimension_semantics=("arbitrary",), vmem_limit_bytes=VMEM_LIMIT_BYTES)


def _mod_kernel(c_ref, w_ref, b_ref, o_ref):
    c = c_ref[...]
    s = c * _sigmoid(c)
    o_ref[...] = jnp.dot(s, w_ref[...], preferred_element_type=F32) + b_ref[...]


def _adaln_mod(c, w_ada, b_ada):
    d = c.shape[-1]
    c8 = jnp.broadcast_to(c, (V7X_SUBLANES, d))
    out = pl.pallas_call(
        _mod_kernel,
        grid=(N_MOD,),
        in_specs=[
            pl.BlockSpec((V7X_SUBLANES, d), lambda i: (0, 0)),
            pl.BlockSpec((d, d), lambda i: (0, i)),
            pl.BlockSpec((1, d), lambda i: (0, i)),
        ],
        out_specs=pl.BlockSpec((V7X_SUBLANES, d), lambda i: (0, i)),
        out_shape=jax.ShapeDtypeStruct((V7X_SUBLANES, N_MOD * d), F32),
        compiler_params=pltpu.CompilerParams(dimension_semantics=("arbitrary",)),
        name="adaln_mod",
    )(c8, w_ada, b_ada.reshape(1, -1))
    return out[0].reshape(N_MOD, d)


def _branches_kernel(x_ref, pos_ref, invf_ref, mod_ref, w_ref, b_ref, cw_ref, cb_ref, clg_ref, clb_ref,
                     wco_ref, gng_ref, gnb_ref, wro_ref, decay_ref, xi_ref, zeta_ref, o_ref,
                     state_ref, abuf_ref, q_ref, qx_ref, k_ref, kz_ref, v_ref, sg_ref, ract_ref,
                     *, tile, d_model, dk, dv, chunk_decay):
    n_heads = d_model // dk
    v_w = n_heads * dv
    n_lane_chunks = d_model // V7X_LANES
    n_chunks = tile // RET_CHUNK
    o_q = 0
    o_k = o_q + d_model
    o_v = o_k + d_model
    o_g = o_v + v_w
    o_cv = o_g + v_w
    o_cg = o_cv + d_model
    o_ga = o_cg + d_model
    o_gb = o_ga + d_model

    @pl.when(pl.program_id(0) == 0)
    def _():
        state_ref[...] = jnp.zeros_like(state_ref)
        abuf_ref[:, 0:CONV_HIST, :] = jnp.zeros((n_lane_chunks, CONV_HIST, V7X_LANES), F32)

    x = x_ref[...]
    h = _ln(x) * (1.0 + mod_ref[1:2, :]) + mod_ref[0:1, :]
    hb = h.astype(BF16)

    def proj(lo, width):
        return jnp.dot(hb, w_ref[:, lo:lo + width], preferred_element_type=F32) + b_ref[:, lo:lo + width]

    a = proj(o_cv, d_model) * _sigmoid(proj(o_cg, d_model))
    for j in range(n_lane_chunks):
        abuf_ref[j, CONV_HIST:CONV_HIST + tile, :] = a[:, j * V7X_LANES:(j + 1) * V7X_LANES]
    first = CONV_HIST - (CONV_KERNEL - 1)
    cols = []
    for j in range(n_lane_chunks):
        ls = slice(j * V7X_LANES, (j + 1) * V7X_LANES)
        blocks = []
        for rb in range(tile // CONV_ROWS):
            acc = jnp.broadcast_to(cb_ref[:, ls], (CONV_ROWS, V7X_LANES))
            for k in range(CONV_KERNEL):
                r0 = rb * CONV_ROWS + first + k
                acc = acc + cw_ref[k:k + 1, ls] * abuf_ref[j, r0:r0 + CONV_ROWS, :]
            blocks.append(acc)
        cols.append(jnp.concatenate(blocks, axis=0))
        abuf_ref[j, 0:CONV_HIST, :] = abuf_ref[j, tile:tile + CONV_HIST, :]
    an = _ln(jnp.concatenate(cols, axis=1)) * clg_ref[...] + clb_ref[...]
    sa = (an * _sigmoid(an)).astype(BF16)

    ang = pos_ref[...] * invf_ref[...]
    cs = jnp.cos(ang)
    sn = jnp.sin(ang)
    lane = lax.broadcasted_iota(jnp.int32, ang.shape, 1)
    sn = jnp.where(lane < dk // 2, -sn, sn)
    k_scale = float(dk) ** -0.5

    def rope_heads(y, cos_t, sin_t):
        for hh in range(n_heads):
            yh = y[:, hh * dk:(hh + 1) * dk]
            yield hh, yh * cos_t + pltpu.roll(yh, dk // 2, 1) * sin_t

    def tiled(tab_ref, hh):
        return jnp.concatenate([tab_ref[:, hh * dk:(hh + 1) * dk]] * n_chunks, axis=0)

    for hh, r in rope_heads(proj(o_q, d_model), cs, sn):
        q_ref[:, hh * dk:(hh + 1) * dk] = r.astype(BF16)
        qx_ref[:, hh * dk:(hh + 1) * dk] = (r * tiled(xi_ref, hh)).astype(BF16)
    for hh, r in rope_heads(proj(o_k, d_model), cs * k_scale, sn * k_scale):
        k_ref[:, hh * dk:(hh + 1) * dk] = r.astype(BF16)
        kz_ref[:, hh * dk:(hh + 1) * dk] = (r * tiled(zeta_ref, hh)).astype(BF16)
    v_ref[...] = proj(o_v, v_w).astype(BF16)
    g = proj(o_g, v_w)
    sg_ref[...] = g * _sigmoid(g)

    for ci in range(n_chunks):
        rows = slice(ci * RET_CHUNK, (ci + 1) * RET_CHUNK)
        for hh in range(n_heads):
            qs = slice(hh * dk, (hh + 1) * dk)
            vs = slice(hh * dv, (hh + 1) * dv)
            vh = v_ref[rows, vs]
            s = lax.dot_general(q_ref[rows, qs], k_ref[rows, qs], (((1,), (1,)), ((), ())),
                                preferred_element_type=F32)
            s = (s * decay_ref[hh]).astype(BF16)
            st = state_ref[hh]
            lhs = jnp.concatenate([s, qx_ref[rows, qs]], axis=1)
            rhs = jnp.concatenate([vh, st.astype(BF16)], axis=0)
            o = jnp.dot(lhs, rhs, preferred_element_type=F32)
            kv = lax.dot_general(kz_ref[rows, qs], vh, (((0,), (0,)), ((), ())),
                                 preferred_element_type=F32)
            state_ref[hh] = st * chunk_decay[hh] + kv
            r = (_ln(o) * gng_ref[:, vs] + gnb_ref[:, vs]) * sg_ref[rows, vs]
            ract_ref[rows, vs] = r.astype(BF16)

    ya = jnp.dot(sa, wco_ref[...], preferred_element_type=F32)
    yb = jnp.dot(ract_ref[...], wro_ref[...], preferred_element_type=F32)
    m = _sigmoid(proj(o_ga, d_model)) * ya + _sigmoid(proj(o_gb, d_model)) * yb
    o_ref[...] = m.astype(BF16)


def _token_branches(x2, pos, mod, w_in, b_in, cw, cb, clg, clb, w_co, gn_g, gn_b, w_ro, tile):
    s_len, d = x2.shape
    dk = d // N_HEADS
    dv = w_ro.shape[0] // N_HEADS
    log_gamma = np.log(1.0 - 2.0 ** (-5.0 - np.arange(N_HEADS, dtype=np.float64)))
    idx = np.arange(RET_CHUNK, dtype=np.float64)
    rel = idx[:, None] - idx[None, :]
    decay = np.where(rel[None] >= 0, np.exp(log_gamma[:, None, None] * np.maximum(rel, 0.0)[None]), 0.0)
    xi = np.repeat(np.exp(log_gamma[None, :] * (idx[:, None] + 1.0)), dk, axis=1)
    zeta = np.repeat(np.exp(log_gamma[None, :] * (RET_CHUNK - 1.0 - idx[:, None])), dk, axis=1)
    chunk_decay = tuple(float(v) for v in np.exp(log_gamma * RET_CHUNK))
    half = dk // 2
    inv_freq = ROPE_BASE ** (-jnp.arange(half, dtype=F32) / half)
    inv_freq = jnp.concatenate([inv_freq, inv_freq]).reshape(1, dk)

    kern = functools.partial(_branches_kernel, tile=tile, d_model=d, dk=dk, dv=dv, chunk_decay=chunk_decay)
    return pl.pallas_call(
        kern,
        grid=(s_len // tile,),
        in_specs=[
            _rows(tile, d),
            _rows(tile, 1),
            _resident((1, dk)),
            _resident((N_MOD, d)),
            _resident(w_in.shape),
            _resident(b_in.shape),
            _resident(cw.shape),
            _resident((1, d)),
            _resident((1, d)),
            _resident((1, d)),
            _resident((d, d)),
            _resident((1, N_HEADS * dv)),
            _resident((1, N_HEADS * dv)),
            _resident((N_HEADS * dv, d)),
            _resident((N_HEADS, RET_CHUNK, RET_CHUNK)),
            _resident((RET_CHUNK, d)),
            _resident((RET_CHUNK, d)),
        ],
        out_specs=_rows(tile, d),
        out_shape=jax.ShapeDtypeStruct((s_len, d), BF16),
        scratch_shapes=[
            pltpu.VMEM((N_HEADS, dk, dv), F32),
            pltpu.VMEM((d // V7X_LANES, CONV_HIST + tile, V7X_LANES), F32),
            pltpu.VMEM((tile, d), BF16),
            pltpu.VMEM((tile, d), BF16),
            pltpu.VMEM((tile, d), BF16),
            pltpu.VMEM((tile, d), BF16),
            pltpu.VMEM((tile, N_HEADS * dv), BF16),
            pltpu.VMEM((tile, N_HEADS * dv), F32),
            pltpu.VMEM((tile, N_HEADS * dv), BF16),
        ],
        compiler_params=_params(),
        name="token_branches",
    )(x2, pos, inv_freq, mod, w_in, b_in, cw, cb, clg, clb, w_co, gn_g, gn_b, w_ro,
      jnp.asarray(decay, F32), jnp.asarray(xi, F32), jnp.asarray(zeta, F32))


def _out_ffn_kernel(x_ref, m_ref, mod_ref, wout_ref, l1g_ref, l1b_ref, wup_ref, fw_ref, fb_ref, wdn_ref,
                    l2g_ref, l2b_ref, o_ref, ubuf_ref, act_ref, *, tile, d_ff):
    n_lane_chunks = 2 * d_ff // V7X_LANES

    @pl.when(pl.program_id(0) == 0)
    def _():
        ubuf_ref[:, 0:FFN_HIST, :] = jnp.zeros((n_lane_chunks, FFN_HIST, V7X_LANES), F32)

    t = jnp.dot(m_ref[...], wout_ref[...], preferred_element_type=F32)
    z1 = DEEPNORM_ALPHA * x_ref[...] + mod_ref[2:3, :] * t
    x1 = _ln(z1) * l1g_ref[...] + l1b_ref[...]

    h = _ln(x1) * (1.0 + mod_ref[4:5, :]) + mod_ref[3:4, :]
    hb = h.astype(BF16)

    def conv_cols(base):
        u = jnp.dot(hb, wup_ref[:, base:base + FFN_COLS], preferred_element_type=F32)
        outs = []
        for jj in range(FFN_COLS // V7X_LANES):
            j = base // V7X_LANES + jj
            ls = slice(j * V7X_LANES, (j + 1) * V7X_LANES)
            uj = u[:, jj * V7X_LANES:(jj + 1) * V7X_LANES]
            ubuf_ref[j, FFN_HIST:FFN_HIST + tile, :] = uj
            y = fb_ref[:, ls] + fw_ref[2:3, ls] * uj
            for k in range(FFN_CONV_KERNEL - 1):
                r0 = FFN_HIST - (FFN_CONV_KERNEL - 1) + k
                y = y + fw_ref[k:k + 1, ls] * ubuf_ref[j, r0:r0 + tile, :]
            ubuf_ref[j, 0:FFN_HIST, :] = ubuf_ref[j, tile:tile + FFN_HIST, :]
            outs.append(y)
        return jnp.concatenate(outs, axis=1)

    for gi in range(d_ff // FFN_COLS):
        val = conv_cols(gi * FFN_COLS)
        gate = conv_cols(d_ff + gi * FFN_COLS)
        act_ref[:, gi * FFN_COLS:(gi + 1) * FFN_COLS] = (val * (gate * _sigmoid(gate))).astype(BF16)

    f = jnp.dot(act_ref[...], wdn_ref[...], preferred_element_type=F32)
    z2 = DEEPNORM_ALPHA * x1 + mod_ref[5:6, :] * f
    o_ref[...] = _ln(z2) * l2g_ref[...] + l2b_ref[...]


def _out_ffn(x2, m, mod, w_out, l1g, l1b, w_up, fw, fb, w_dn, l2g, l2b, tile):
    s_len, d = x2.shape
    d_ff = w_dn.shape[0]
    kern = functools.partial(_out_ffn_kernel, tile=tile, d_ff=d_ff)
    return pl.pallas_call(
        kern,
        grid=(s_len // tile,),
        in_specs=[
            _rows(tile, d),
            _rows(tile, d),
            _resident((N_MOD, d)),
            _resident((d, d)),
            _resident((1, d)),
            _resident((1, d)),
            _resident(w_up.shape),
            _resident(fw.shape),
            _resident((1, 2 * d_ff)),
            _resident(w_dn.shape),
            _resident((1, d)),
            _resident((1, d)),
        ],
        out_specs=_rows(tile, d),
        out_shape=jax.ShapeDtypeStruct((s_len, d), F32),
        scratch_shapes=[
            pltpu.VMEM((2 * d_ff // V7X_LANES, FFN_HIST + tile, V7X_LANES), F32),
            pltpu.VMEM((tile, d_ff), BF16),
        ],
        compiler_params=_params(),
        name="out_ffn",
    )(x2, m, mod, w_out, l1g, l1b, w_up, fw, fb, w_dn, l2g, l2b)


def kernel(x, c, positions, w_ada, b_ada, w_in, b_in, conv_dw_w, conv_dw_b, conv_ln_g, conv_ln_b,
           w_conv_out, ret_gn_g, ret_gn_b, w_ret_out, w_out, ln1_g, ln1_b, w_up, ffn_dw_w, ffn_dw_b,
           w_down, ln2_g, ln2_b):
    bsz, s_len, d = x.shape
    assert bsz == 1 and s_len % SEQ_TILE == 0 and d % V7X_LANES == 0
    depth = w_ada.shape[0]

    row = lambda v: v.reshape(1, -1)
    x2 = x.reshape(s_len, d)
    pos = positions.reshape(s_len, 1).astype(F32)
    for l in range(depth):
        mod = _adaln_mod(c, w_ada[l], b_ada[l])
        m = _token_branches(x2, pos, mod, w_in[l].astype(BF16), row(b_in[l]), conv_dw_w[l], row(conv_dw_b[l]),
                            row(conv_ln_g[l]), row(conv_ln_b[l]), w_conv_out[l].astype(BF16),
                            row(ret_gn_g[l]), row(ret_gn_b[l]), w_ret_out[l].astype(BF16), SEQ_TILE)
        x2 = _out_ffn(x2, m, mod, w_out[l].astype(BF16), row(ln1_g[l]), row(ln1_b[l]), w_up[l].astype(BF16),
                      ffn_dw_w[l], row(ffn_dw_b[l]), w_down[l].astype(BF16), row(ln2_g[l]), row(ln2_b[l]),
                      SEQ_TILE)
    return x2.reshape(bsz, s_len, d)
```

```python
import functools

import jax
import jax.numpy as jnp
import numpy as np
from jax import lax
from jax.experimental import pallas as pl
from jax.experimental.pallas import tpu as pltpu

F32 = jnp.float32
BF16 = jnp.bfloat16

DEPTH = 1
N_HEADS = 8
RET_CHUNK = 128
CONV_KERNEL = 31
FFN_CONV_KERNEL = 3
ROPE_BASE = 10000.0
LN_EPS = 1e-5
DEEPNORM_ALPHA = (2.0 * DEPTH) ** 0.25
N_MOD = 6

V7X_LANES = 128
V7X_SUBLANES = 8

SEQ_TILE = 512
CONV_HIST = 32
FFN_HIST = V7X_SUBLANES
CONV_ROWS = 128
FFN_COLS = 256
VMEM_LIMIT_BYTES = 62 * 1024 * 1024


def _ln(x):
    mu = jnp.mean(x, axis=-1, keepdims=True)
    xc = x - mu
    var = jnp.mean(xc * xc, axis=-1, keepdims=True)
    return xc * lax.rsqrt(var + LN_EPS)


def _sigmoid(x):
    return jax.nn.sigmoid(x)


def _resident(shape):
    nd = len(shape)
    return pl.BlockSpec(shape, lambda i: (0,) * nd, pipeline_mode=pl.Buffered(1))


def _rows(tile, width):
    return pl.BlockSpec((tile, width), lambda i: (i, 0))


def _params():
    return pltpu.CompilerParams(dimension_semantics=("arbitrary",), vmem_limit_bytes=VMEM_LIMIT_BYTES)


def _mod_kernel(c_ref, w_ref, b_ref, o_ref):
    c = c_ref[...]
    s = c * _sigmoid(c)
    o_ref[...] = jnp.dot(s, w_ref[...], preferred_element_type=F32) + b_ref[...]


def _adaln_mod(c, w_ada, b_ada):
    d = c.shape[-1]
    c8 = jnp.broadcast_to(c, (V7X_SUBLANES, d))
    out = pl.pallas_call(
        _mod_kernel,
        grid=(N_MOD,),
        in_specs=[
            pl.BlockSpec((V7X_SUBLANES, d), lambda i: (0, 0)),
            pl.BlockSpec((d, d), lambda i: (0, i)),
            pl.BlockSpec((1, d), lambda i: (0, i)),
        ],
        out_specs=pl.BlockSpec((V7X_SUBLANES, d), lambda i: (0, i)),
        out_shape=jax.ShapeDtypeStruct((V7X_SUBLANES, N_MOD * d), F32),
        compiler_params=pltpu.CompilerParams(dimension_semantics=("arbitrary",)),
        name="adaln_mod",
    )(c8, w_ada, b_ada.reshape(1, -1))
    return out[0].reshape(N_MOD, d)


def _branches_kernel(x_ref, pos_ref, invf_ref, mod_ref, w_ref, b_ref, cw_ref, cb_ref, clg_ref, clb_ref,
                     wco_ref, gng_ref, gnb_ref, wro_ref, decay_ref, xi_ref, zeta_ref, o_ref,
                     state_ref, abuf_ref, q_ref, qx_ref, k_ref, kz_ref, v_ref, sg_ref, ract_ref,
                     *, tile, d_model, dk, dv, chunk_decay):
    n_heads = d_model // dk
    v_w = n_heads * dv
    n_lane_chunks = d_model // V7X_LANES
    n_chunks = tile // RET_CHUNK
    o_q = 0
    o_k = o_q + d_model
    o_v = o_k + d_model
    o_g = o_v + v_w
    o_cv = o_g + v_w
    o_cg = o_cv + d_model
    o_ga = o_cg + d_model
    o_gb = o_ga + d_model

    @pl.when(pl.program_id(0) == 0)
    def _():
        state_ref[...] = jnp.zeros_like(state_ref)
        abuf_ref[:, 0:CONV_HIST, :] = jnp.zeros((n_lane_chunks, CONV_HIST, V7X_LANES), F32)

    x = x_ref[...]
    h = _ln(x) * (1.0 + mod_ref[1:2, :]) + mod_ref[0:1, :]
    hb = h.astype(BF16)

    def proj(lo, width):
        return jnp.dot(hb, w_ref[:, lo:lo + width], preferred_element_type=F32) + b_ref[:, lo:lo + width]

    def anchor(v):
        bits = lax.bitcast_convert_type(v[0:1, 0:V7X_LANES].astype(F32), jnp.uint32)
        bits = lax.shift_right_logical(lax.shift_right_logical(bits, jnp.uint32(16)), jnp.uint32(16))
        return lax.bitcast_convert_type(bits, F32)

    a = proj(o_cv, d_model) * _sigmoid(proj(o_cg, d_model))
    for j in range(n_lane_chunks):
        abuf_ref[j, CONV_HIST:CONV_HIST + tile, :] = a[:, j * V7X_LANES:(j + 1) * V7X_LANES]
    first = CONV_HIST - (CONV_KERNEL - 1)

    def conv_chunk(j, zero):
        ls = slice(j * V7X_LANES, (j + 1) * V7X_LANES)
        bias = cb_ref[:, ls] if zero is None else cb_ref[:, ls] + zero
        blocks = []
        for rb in range(tile // CONV_ROWS):
            acc = jnp.broadcast_to(bias, (CONV_ROWS, V7X_LANES))
            for k in range(CONV_KERNEL):
                r0 = rb * CONV_ROWS + first + k
                acc = acc + cw_ref[k:k + 1, ls] * abuf_ref[j, r0:r0 + CONV_ROWS, :]
            blocks.append(acc)
        abuf_ref[j, 0:CONV_HIST, :] = abuf_ref[j, tile:tile + CONV_HIST, :]
        return jnp.concatenate(blocks, axis=0)

    ang = pos_ref[...] * invf_ref[...]
    cs = jnp.cos(ang)
    sn = jnp.sin(ang)
    lane = lax.broadcasted_iota(jnp.int32, ang.shape, 1)
    sn = jnp.where(lane < dk // 2, -sn, sn)
    k_scale = float(dk) ** -0.5

    def rope_heads(y, cos_t, sin_t):
        for hh in range(n_heads):
            yh = y[:, hh * dk:(hh + 1) * dk]
            yield hh, yh * cos_t + pltpu.roll(yh, dk // 2, 1) * sin_t

    def tiled(tab_ref, hh):
        return jnp.concatenate([tab_ref[:, hh * dk:(hh + 1) * dk]] * n_chunks, axis=0)

    half_v = v_w // 2
    vb0 = proj(o_v, half_v).astype(BF16)
    v_ref[:, 0:half_v] = vb0
    vb1 = proj(o_v + half_v, half_v).astype(BF16)
    v_ref[:, half_v:v_w] = vb1
    pins = {3: anchor(vb0), 6: anchor(vb1)}
    cols = [conv_chunk(j, pins.get(j)) for j in range(n_lane_chunks)]
    for hh, r in rope_heads(proj(o_q, d_model), cs, sn):
        q_ref[:, hh * dk:(hh + 1) * dk] = r.astype(BF16)
        qx_ref[:, hh * dk:(hh + 1) * dk] = (r * tiled(xi_ref, hh)).astype(BF16)
    for hh, r in rope_heads(proj(o_k, d_model), cs * k_scale, sn * k_scale):
        k_ref[:, hh * dk:(hh + 1) * dk] = r.astype(BF16)
        kz_ref[:, hh * dk:(hh + 1) * dk] = (r * tiled(zeta_ref, hh)).astype(BF16)
    g = proj(o_g, half_v)
    sg_ref[:, 0:half_v] = g * _sigmoid(g)
    g = proj(o_g + half_v, half_v)
    sg_ref[:, half_v:v_w] = g * _sigmoid(g)
    ga = _sigmoid(proj(o_ga, d_model))
    gb = _sigmoid(proj(o_gb, d_model))

    an = _ln(jnp.concatenate(cols, axis=1)) * clg_ref[...] + clb_ref[...]
    sa = (an * _sigmoid(an)).astype(BF16)
    ua = ga * jnp.dot(sa, wco_ref[...], preferred_element_type=F32)

    heads = range(n_heads)
    for ci in range(n_chunks):
        rows = slice(ci * RET_CHUNK, (ci + 1) * RET_CHUNK)
        qs = [slice(hh * dk, (hh + 1) * dk) for hh in heads]
        vs = [slice(hh * dv, (hh + 1) * dv) for hh in heads]
        sc = [lax.dot_general(q_ref[rows, qs[hh]], k_ref[rows, qs[hh]], (((1,), (1,)), ((), ())),
                              preferred_element_type=F32) for hh in heads]
        lhs = [jnp.concatenate([(sc[hh] * decay_ref[hh]).astype(BF16), qx_ref[rows, qs[hh]]], axis=1)
               for hh in heads]
        st = [state_ref[hh] for hh in heads]
        vh = [v_ref[rows, vs[hh]] for hh in heads]
        o = [jnp.dot(lhs[hh], jnp.concatenate([vh[hh], st[hh].astype(BF16)], axis=0),
                     preferred_element_type=F32) for hh in heads]
        kv = [lax.dot_general(kz_ref[rows, qs[hh]], vh[hh], (((0,), (0,)), ((), ())),
                              preferred_element_type=F32) for hh in heads]
        for hh in heads:
            state_ref[hh] = st[hh] * chunk_decay[hh] + kv[hh]
        for hh in heads:
            r = (_ln(o[hh]) * gng_ref[:, vs[hh]] + gnb_ref[:, vs[hh]]) * sg_ref[rows, vs[hh]]
            ract_ref[rows, vs[hh]] = r.astype(BF16)

    yb = jnp.dot(ract_ref[...], wro_ref[...], preferred_element_type=F32)
    o_ref[...] = (ua + gb * yb).astype(BF16)


def _token_branches(x2, pos, mod, w_in, b_in, cw, cb, clg, clb, w_co, gn_g, gn_b, w_ro, tile):
    s_len, d = x2.shape
    dk = d // N_HEADS
    dv = w_ro.shape[0] // N_HEADS
    log_gamma = np.log(1.0 - 2.0 ** (-5.0 - np.arange(N_HEADS, dtype=np.float64)))
    idx = np.arange(RET_CHUNK, dtype=np.float64)
    rel = idx[:, None] - idx[None, :]
    decay = np.where(rel[None] >= 0, np.exp(log_gamma[:, None, None] * np.maximum(rel, 0.0)[None]), 0.0)
    xi = np.repeat(np.exp(log_gamma[None, :] * (idx[:, None] + 1.0)), dk, axis=1)
    zeta = np.repeat(np.exp(log_gamma[None, :] * (RET_CHUNK - 1.0 - idx[:, None])), dk, axis=1)
    chunk_decay = tuple(float(v) for v in np.exp(log_gamma * RET_CHUNK))
    half = dk // 2
    inv_freq = ROPE_BASE ** (-jnp.arange(half, dtype=F32) / half)
    inv_freq = jnp.concatenate([inv_freq, inv_freq]).reshape(1, dk)

    kern = functools.partial(_branches_kernel, tile=tile, d_model=d, dk=dk, dv=dv, chunk_decay=chunk_decay)
    return pl.pallas_call(
        kern,
        grid=(s_len // tile,),
        in_specs=[
            _rows(tile, d),
            _rows(tile, 1),
            _resident((1, dk)),
            _resident((N_MOD, d)),
            _resident(w_in.shape),
            _resident(b_in.shape),
            _resident(cw.shape),
            _resident((1, d)),
            _resident((1, d)),
            _resident((1, d)),
            _resident((d, d)),
            _resident((1, N_HEADS * dv)),
            _resident((1, N_HEADS * dv)),
            _resident((N_HEADS * dv, d)),
            _resident((N_HEADS, RET_CHUNK, RET_CHUNK)),
            _resident((RET_CHUNK, d)),
            _resident((RET_CHUNK, d)),
        ],
        out_specs=_rows(tile, d),
        out_shape=jax.ShapeDtypeStruct((s_len, d), BF16),
        scratch_shapes=[
            pltpu.VMEM((N_HEADS, dk, dv), F32),
            pltpu.VMEM((d // V7X_LANES, CONV_HIST + tile, V7X_LANES), F32),
            pltpu.VMEM((tile, d), BF16),
            pltpu.VMEM((tile, d), BF16),
            pltpu.VMEM((tile, d), BF16),
            pltpu.VMEM((tile, d), BF16),
            pltpu.VMEM((tile, N_HEADS * dv), BF16),
            pltpu.VMEM((tile, N_HEADS * dv), F32),
            pltpu.VMEM((tile, N_HEADS * dv), BF16),
        ],
        compiler_params=_params(),
        name="token_branches",
    )(x2, pos, inv_freq, mod, w_in, b_in, cw, cb, clg, clb, w_co, gn_g, gn_b, w_ro,
      jnp.asarray(decay, F32), jnp.asarray(xi, F32), jnp.asarray(zeta, F32))


def _out_ffn_kernel(x_ref, m_ref, mod_ref, wout_ref, l1g_ref, l1b_ref, wup_ref, fw_ref, fb_ref, wdn_ref,
                    l2g_ref, l2b_ref, o_ref, ubuf_ref, act_ref, *, tile, d_ff):
    n_lane_chunks = 2 * d_ff // V7X_LANES

    @pl.when(pl.program_id(0) == 0)
    def _():
        ubuf_ref[:, 0:FFN_HIST, :] = jnp.zeros((n_lane_chunks, FFN_HIST, V7X_LANES), F32)

    t = jnp.dot(m_ref[...], wout_ref[...], preferred_element_type=F32)
    z1 = DEEPNORM_ALPHA * x_ref[...] + mod_ref[2:3, :] * t
    x1 = _ln(z1) * l1g_ref[...] + l1b_ref[...]

    h = _ln(x1) * (1.0 + mod_ref[4:5, :]) + mod_ref[3:4, :]
    hb = h.astype(BF16)

    def conv_cols(base):
        u = jnp.dot(hb, wup_ref[:, base:base + FFN_COLS], preferred_element_type=F32)
        outs = []
        for jj in range(FFN_COLS // V7X_LANES):
            j = base // V7X_LANES + jj
            ls = slice(j * V7X_LANES, (j + 1) * V7X_LANES)
            uj = u[:, jj * V7X_LANES:(jj + 1) * V7X_LANES]
            ubuf_ref[j, FFN_HIST:FFN_HIST + tile, :] = uj
            y = fb_ref[:, ls] + fw_ref[2:3, ls] * uj
            for k in range(FFN_CONV_KERNEL - 1):
                r0 = FFN_HIST - (FFN_CONV_KERNEL - 1) + k
                y = y + fw_ref[k:k + 1, ls] * ubuf_ref[j, r0:r0 + tile, :]
            ubuf_ref[j, 0:FFN_HIST, :] = ubuf_ref[j, tile:tile + FFN_HIST, :]
            outs.append(y)
        return jnp.concatenate(outs, axis=1)

    for gi in range(d_ff // FFN_COLS):
        val = conv_cols(gi * FFN_COLS)
        gate = conv_cols(d_ff + gi * FFN_COLS)
        act_ref[:, gi * FFN_COLS:(gi + 1) * FFN_COLS] = (val * (gate * _sigmoid(gate))).astype(BF16)

    f = jnp.dot(act_ref[...], wdn_ref[...], preferred_element_type=F32)
    z2 = DEEPNORM_ALPHA * x1 + mod_ref[5:6, :] * f
    o_ref[...] = _ln(z2) * l2g_ref[...] + l2b_ref[...]


def _out_ffn(x2, m, mod, w_out, l1g, l1b, w_up, fw, fb, w_dn, l2g, l2b, tile):
    s_len, d = x2.shape
    d_ff = w_dn.shape[0]
    kern = functools.partial(_out_ffn_kernel, tile=tile, d_ff=d_ff)
    return pl.pallas_call(
        kern,
        grid=(s_len // tile,),
        in_specs=[
            _rows(tile, d),
            _rows(tile, d),
            _resident((N_MOD, d)),
            _resident((d, d)),
            _resident((1, d)),
            _resident((1, d)),
            _resident(w_up.shape),
            _resident(fw.shape),
            _resident((1, 2 * d_ff)),
            _resident(w_dn.shape),
            _resident((1, d)),
            _resident((1, d)),
        ],
        out_specs=_rows(tile, d),
        out_shape=jax.ShapeDtypeStruct((s_len, d), F32),
        scratch_shapes=[
            pltpu.VMEM((2 * d_ff // V7X_LANES, FFN_HIST + tile, V7X_LANES), F32),
            pltpu.VMEM((tile, d_ff), BF16),
        ],
        compiler_params=_params(),
        name="out_ffn",
    )(x2, m, mod, w_out, l1g, l1b, w_up, fw, fb, w_dn, l2g, l2b)


def kernel(x, c, positions, w_ada, b_ada, w_in, b_in, conv_dw_w, conv_dw_b, conv_ln_g, conv_ln_b,
           w_conv_out, ret_gn_g, ret_gn_b, w_ret_out, w_out, ln1_g, ln1_b, w_up, ffn_dw_w, ffn_dw_b,
           w_down, ln2_g, ln2_b):
    bsz, s_len, d = x.shape
    assert bsz == 1 and s_len % SEQ_TILE == 0 and d % V7X_LANES == 0
    depth = w_ada.shape[0]

    row = lambda v: v.reshape(1, -1)
    x2 = x.reshape(s_len, d)
    pos = positions.reshape(s_len, 1).astype(F32)
    for l in range(depth):
        mod = _adaln_mod(c, w_ada[l], b_ada[l])
        m = _token_branches(x2, pos, mod, w_in[l].astype(BF16), row(b_in[l]), conv_dw_w[l], row(conv_dw_b[l]),
                            row(conv_ln_g[l]), row(conv_ln_b[l]), w_conv_out[l].astype(BF16),
                            row(ret_gn_g[l]), row(ret_gn_b[l]), w_ret_out[l].astype(BF16), SEQ_TILE)
        x2 = _out_ffn(x2, m, mod, w_out[l].astype(BF16), row(ln1_g[l]), row(ln1_b[l]), w_up[l].astype(BF16),
                      ffn_dw_w[l], row(ffn_dw_b[l]), w_down[l].astype(BF16), row(ln2_g[l]), row(ln2_b[l]),
                      SEQ_TILE)
    return x2.reshape(bsz, s_len, d)
```

```python
import functools

import jax
import jax.numpy as jnp
import numpy as np
from jax import lax
from jax.experimental import pallas as pl
from jax.experimental.pallas import tpu as pltpu

F32 = jnp.float32
BF16 = jnp.bfloat16

DEPTH = 1
N_HEADS = 8
RET_CHUNK = 128
CONV_KERNEL = 31
FFN_CONV_KERNEL = 3
ROPE_BASE = 10000.0
LN_EPS = 1e-5
DEEPNORM_ALPHA = (2.0 * DEPTH) ** 0.25
N_MOD = 6

V7X_LANES = 128
V7X_SUBLANES = 8

SEQ_TILE = 512
CONV_HIST = 32
FFN_HIST = V7X_SUBLANES
CONV_ROWS = 64
FFN_COLS = 256
VG_COLS = 512
VMEM_LIMIT_BYTES = 62 * 1024 * 1024


def _ln(x):
    mu = jnp.mean(x, axis=-1, keepdims=True)
    xc = x - mu
    var = jnp.mean(xc * xc, axis=-1, keepdims=True)
    return xc * lax.rsqrt(var + LN_EPS)


def _sigmoid(x):
    return jax.nn.sigmoid(x)


def _resident(shape):
    nd = len(shape)
    return pl.BlockSpec(shape, lambda i: (0,) * nd, pipeline_mode=pl.Buffered(1))


def _rows(tile, width):
    return pl.BlockSpec((tile, width), lambda i: (i, 0))


def _params():
    return pltpu.CompilerParams(dimension_semantics=("arbitrary",), vmem_limit_bytes=VMEM_LIMIT_BYTES)


def _mod_kernel(c_ref, w_ref, b_ref, o_ref):
    c = c_ref[...]
    s = c * _sigmoid(c)
    o_ref[...] = jnp.dot(s, w_ref[...], preferred_element_type=F32) + b_ref[...]


def _adaln_mod(c, w_ada, b_ada):
    d = c.shape[-1]
    c8 = jnp.broadcast_to(c, (V7X_SUBLANES, d))
    out = pl.pallas_call(
        _mod_kernel,
        grid=(N_MOD,),
        in_specs=[
            pl.BlockSpec((V7X_SUBLANES, d), lambda i: (0, 0)),
            pl.BlockSpec((d, d), lambda i: (0, i)),
            pl.BlockSpec((1, d), lambda i: (0, i)),
        ],
        out_specs=pl.BlockSpec((V7X_SUBLANES, d), lambda i: (0, i)),
        out_shape=jax.ShapeDtypeStruct((V7X_SUBLANES, N_MOD * d), F32),
        compiler_params=pltpu.CompilerParams(dimension_semantics=("arbitrary",)),
        name="adaln_mod",
    )(c8, w_ada, b_ada.reshape(1, -1))
    return out[0].reshape(N_MOD, d)


def _branches_kernel(x_ref, pos_ref, invf_ref, mod_ref, w_ref, b_ref, wvg_ref, bvg_ref, cw_ref, cb_ref, clg_ref,
                     clb_ref, wco_ref, gng_ref, gnb_ref, wro_ref, decay_ref, xi_ref, zeta_ref, o_ref,
                     state_ref, abuf_ref, cbuf_ref, hb_ref, q_ref, qx_ref, k_ref, kz_ref, v_ref, sg_ref, ract_ref,
                     *, tile, d_model, dk, dv, chunk_decay):
    n_heads = d_model // dk
    v_w = n_heads * dv
    n_lane_chunks = d_model // V7X_LANES
    n_chunks = tile // RET_CHUNK
    o_q, o_k, o_cv, o_cg, o_ga, o_gb = (i * d_model for i in range(6))
    n_vg = v_w // VG_COLS
    assert n_lane_chunks == 2 * n_vg and VG_COLS % dv == 0

    @pl.when(pl.program_id(0) == 0)
    def _():
        state_ref[...] = jnp.zeros_like(state_ref)
        abuf_ref[:, 0:CONV_HIST, :] = jnp.zeros((n_lane_chunks, CONV_HIST, V7X_LANES), F32)

    x = x_ref[...]
    h = _ln(x) * (1.0 + mod_ref[1:2, :]) + mod_ref[0:1, :]
    hb = h.astype(BF16)
    hb_ref[...] = hb

    def proj(lo, width):
        return jnp.dot(hb, w_ref[:, lo:lo + width], preferred_element_type=F32) + b_ref[:, lo:lo + width]

    a = proj(o_cv, d_model) * _sigmoid(proj(o_cg, d_model))
    for j in range(n_lane_chunks):
        abuf_ref[j, CONV_HIST:CONV_HIST + tile, :] = a[:, j * V7X_LANES:(j + 1) * V7X_LANES]
    first = CONV_HIST - (CONV_KERNEL - 1)
    n_sub = CONV_ROWS // V7X_SUBLANES

    def conv_chunk(j):
        for rb in range(tile // CONV_ROWS):
            acc = jnp.broadcast_to(cb_ref[j], (CONV_ROWS, V7X_LANES))
            base = rb * CONV_ROWS + first
            for phase in range(V7X_SUBLANES):
                taps = [k for k in range(CONV_KERNEL) if (first + k) % V7X_SUBLANES == phase]
                start = base + taps[0]
                span = [abuf_ref[j, start + V7X_SUBLANES * i:start + V7X_SUBLANES * (i + 1), :]
                        for i in range(n_sub + (taps[-1] - taps[0]) // V7X_SUBLANES)]
                for k in taps:
                    m = (k - taps[0]) // V7X_SUBLANES
                    acc = acc + cw_ref[j, k:k + 1, :] * jnp.concatenate(span[m:m + n_sub], axis=0)
            cbuf_ref[j, rb * CONV_ROWS:(rb + 1) * CONV_ROWS, :] = acc
        abuf_ref[j, 0:CONV_HIST, :] = abuf_ref[j, tile:tile + CONV_HIST, :]

    def v_step(it, carry):
        conv_chunk(it)
        y = jnp.dot(hb_ref[...], wvg_ref[it], preferred_element_type=F32) + bvg_ref[it]
        v_ref[it] = y.astype(BF16)
        return carry

    def g_step(it, carry):
        conv_chunk(n_vg + it)
        y = jnp.dot(hb_ref[...], wvg_ref[n_vg + it], preferred_element_type=F32) + bvg_ref[n_vg + it]
        sg_ref[it] = y * _sigmoid(y)
        return carry

    lax.fori_loop(0, n_vg, v_step, 0)
    lax.fori_loop(0, n_vg, g_step, 0)

    ang = pos_ref[...] * invf_ref[...]
    cs = jnp.cos(ang)
    sn = jnp.sin(ang)
    lane = lax.broadcasted_iota(jnp.int32, ang.shape, 1)
    sn = jnp.where(lane < dk // 2, -sn, sn)
    k_scale = float(dk) ** -0.5

    def rope_heads(y, cos_t, sin_t):
        for hh in range(n_heads):
            yh = y[:, hh * dk:(hh + 1) * dk]
            yield hh, yh * cos_t + pltpu.roll(yh, dk // 2, 1) * sin_t

    def tiled(tab_ref, hh):
        return jnp.concatenate([tab_ref[:, hh * dk:(hh + 1) * dk]] * n_chunks, axis=0)

    for hh, r in rope_heads(proj(o_q, d_model), cs, sn):
        q_ref[:, hh * dk:(hh + 1) * dk] = r.astype(BF16)
        qx_ref[:, hh * dk:(hh + 1) * dk] = (r * tiled(xi_ref, hh)).astype(BF16)
    for hh, r in rope_heads(proj(o_k, d_model), cs * k_scale, sn * k_scale):
        k_ref[:, hh * dk:(hh + 1) * dk] = r.astype(BF16)
        kz_ref[:, hh * dk:(hh + 1) * dk] = (r * tiled(zeta_ref, hh)).astype(BF16)
    ga = _sigmoid(proj(o_ga, d_model))
    gb = _sigmoid(proj(o_gb, d_model))

    conv_out = jnp.concatenate([cbuf_ref[j] for j in range(n_lane_chunks)], axis=1)
    an = _ln(conv_out) * clg_ref[...] + clb_ref[...]
    sa = (an * _sigmoid(an)).astype(BF16)
    ua = ga * jnp.dot(sa, wco_ref[...], preferred_element_type=F32)

    heads = range(n_heads)
    qs = [slice(hh * dk, (hh + 1) * dk) for hh in heads]
    vs = [slice(hh * dv, (hh + 1) * dv) for hh in heads]
    piece = [(hh * dv) // VG_COLS for hh in heads]
    vsl = [slice((hh * dv) % VG_COLS, (hh * dv) % VG_COLS + dv) for hh in heads]
    for ci in range(n_chunks):
        rows = slice(ci * RET_CHUNK, (ci + 1) * RET_CHUNK)
        sc = [lax.dot_general(q_ref[rows, qs[hh]], k_ref[rows, qs[hh]], (((1,), (1,)), ((), ())),
                              preferred_element_type=F32) for hh in heads]
        lhs = [jnp.concatenate([(sc[hh] * decay_ref[hh]).astype(BF16), qx_ref[rows, qs[hh]]], axis=1)
               for hh in heads]
        st = [state_ref[hh] for hh in heads]
        vh = [v_ref[piece[hh], rows, vsl[hh]] for hh in heads]
        o = [jnp.dot(lhs[hh], jnp.concatenate([vh[hh], st[hh].astype(BF16)], axis=0),
                     preferred_element_type=F32) for hh in heads]
        kv = [lax.dot_general(kz_ref[rows, qs[hh]], vh[hh], (((0,), (0,)), ((), ())),
                              preferred_element_type=F32) for hh in heads]
        for hh in heads:
            state_ref[hh] = st[hh] * chunk_decay[hh] + kv[hh]
        for hh in heads:
            r = (_ln(o[hh]) * gng_ref[:, vs[hh]] + gnb_ref[:, vs[hh]]) * sg_ref[piece[hh], rows, vsl[hh]]
            ract_ref[rows, vs[hh]] = r.astype(BF16)

    yb = jnp.dot(ract_ref[...], wro_ref[...], preferred_element_type=F32)
    o_ref[...] = (ua + gb * yb).astype(BF16)


def _token_branches(x2, pos, mod, w_main, b_main, w_vg, b_vg, cw, cb, clg, clb, w_co, gn_g, gn_b, w_ro, tile):
    s_len, d = x2.shape
    dk = d // N_HEADS
    dv = w_ro.shape[0] // N_HEADS
    log_gamma = np.log(1.0 - 2.0 ** (-5.0 - np.arange(N_HEADS, dtype=np.float64)))
    idx = np.arange(RET_CHUNK, dtype=np.float64)
    rel = idx[:, None] - idx[None, :]
    decay = np.where(rel[None] >= 0, np.exp(log_gamma[:, None, None] * np.maximum(rel, 0.0)[None]), 0.0)
    xi = np.repeat(np.exp(log_gamma[None, :] * (idx[:, None] + 1.0)), dk, axis=1)
    zeta = np.repeat(np.exp(log_gamma[None, :] * (RET_CHUNK - 1.0 - idx[:, None])), dk, axis=1)
    chunk_decay = tuple(float(v) for v in np.exp(log_gamma * RET_CHUNK))
    half = dk // 2
    inv_freq = ROPE_BASE ** (-jnp.arange(half, dtype=F32) / half)
    inv_freq = jnp.concatenate([inv_freq, inv_freq]).reshape(1, dk)

    kern = functools.partial(_branches_kernel, tile=tile, d_model=d, dk=dk, dv=dv, chunk_decay=chunk_decay)
    return pl.pallas_call(
        kern,
        grid=(s_len // tile,),
        in_specs=[
            _rows(tile, d),
            _rows(tile, 1),
            _resident((1, dk)),
            _resident((N_MOD, d)),
            _resident(w_main.shape),
            _resident(b_main.shape),
            _resident(w_vg.shape),
            _resident(b_vg.shape),
            _resident(cw.shape),
            _resident(cb.shape),
            _resident((1, d)),
            _resident((1, d)),
            _resident((d, d)),
            _resident((1, N_HEADS * dv)),
            _resident((1, N_HEADS * dv)),
            _resident((N_HEADS * dv, d)),
            _resident((N_HEADS, RET_CHUNK, RET_CHUNK)),
            _resident((RET_CHUNK, d)),
            _resident((RET_CHUNK, d)),
        ],
        out_specs=_rows(tile, d),
        out_shape=jax.ShapeDtypeStruct((s_len, d), BF16),
        scratch_shapes=[
            pltpu.VMEM((N_HEADS, dk, dv), F32),
            pltpu.VMEM((d // V7X_LANES, CONV_HIST + tile, V7X_LANES), F32),
            pltpu.VMEM((d // V7X_LANES, tile, V7X_LANES), F32),
            pltpu.VMEM((tile, d), BF16),
            pltpu.VMEM((tile, d), BF16),
            pltpu.VMEM((tile, d), BF16),
            pltpu.VMEM((tile, d), BF16),
            pltpu.VMEM((tile, d), BF16),
            pltpu.VMEM((N_HEADS * dv // VG_COLS, tile, VG_COLS), BF16),
            pltpu.VMEM((N_HEADS * dv // VG_COLS, tile, VG_COLS), F32),
            pltpu.VMEM((tile, N_HEADS * dv), BF16),
        ],
        compiler_params=_params(),
        name="token_branches",
    )(x2, pos, inv_freq, mod, w_main, b_main, w_vg, b_vg, cw, cb, clg, clb, w_co, gn_g, gn_b, w_ro,
      jnp.asarray(decay, F32), jnp.asarray(xi, F32), jnp.asarray(zeta, F32))


def _out_ffn_kernel(x_ref, m_ref, mod_ref, wout_ref, l1g_ref, l1b_ref, wup_ref, fw_ref, fb_ref, wdn_ref,
                    l2g_ref, l2b_ref, o_ref, ubuf_ref, act_ref, *, tile, d_ff):
    n_lane_chunks = 2 * d_ff // V7X_LANES

    @pl.when(pl.program_id(0) == 0)
    def _():
        ubuf_ref[:, 0:FFN_HIST, :] = jnp.zeros((n_lane_chunks, FFN_HIST, V7X_LANES), F32)

    t = jnp.dot(m_ref[...], wout_ref[...], preferred_element_type=F32)
    z1 = DEEPNORM_ALPHA * x_ref[...] + mod_ref[2:3, :] * t
    x1 = _ln(z1) * l1g_ref[...] + l1b_ref[...]

    h = _ln(x1) * (1.0 + mod_ref[4:5, :]) + mod_ref[3:4, :]
    hb = h.astype(BF16)

    def conv_cols(base):
        u = jnp.dot(hb, wup_ref[:, base:base + FFN_COLS], preferred_element_type=F32)
        outs = []
        for jj in range(FFN_COLS // V7X_LANES):
            j = base // V7X_LANES + jj
            ls = slice(j * V7X_LANES, (j + 1) * V7X_LANES)
            uj = u[:, jj * V7X_LANES:(jj + 1) * V7X_LANES]
            ubuf_ref[j, FFN_HIST:FFN_HIST + tile, :] = uj
            y = fb_ref[:, ls] + fw_ref[2:3, ls] * uj
            for k in range(FFN_CONV_KERNEL - 1):
                r0 = FFN_HIST - (FFN_CONV_KERNEL - 1) + k
                y = y + fw_ref[k:k + 1, ls] * ubuf_ref[j, r0:r0 + tile, :]
            ubuf_ref[j, 0:FFN_HIST, :] = ubuf_ref[j, tile:tile + FFN_HIST, :]
            outs.append(y)
        return jnp.concatenate(outs, axis=1)

    for gi in range(d_ff // FFN_COLS):
        val = conv_cols(gi * FFN_COLS)
        gate = conv_cols(d_ff + gi * FFN_COLS)
        act_ref[:, gi * FFN_COLS:(gi + 1) * FFN_COLS] = (val * (gate * _sigmoid(gate))).astype(BF16)

    f = jnp.dot(act_ref[...], wdn_ref[...], preferred_element_type=F32)
    z2 = DEEPNORM_ALPHA * x1 + mod_ref[5:6, :] * f
    o_ref[...] = _ln(z2) * l2g_ref[...] + l2b_ref[...]


def _out_ffn(x2, m, mod, w_out, l1g, l1b, w_up, fw, fb, w_dn, l2g, l2b, tile):
    s_len, d = x2.shape
    d_ff = w_dn.shape[0]
    kern = functools.partial(_out_ffn_kernel, tile=tile, d_ff=d_ff)
    return pl.pallas_call(
        kern,
        grid=(s_len // tile,),
        in_specs=[
            _rows(tile, d),
            _rows(tile, d),
            _resident((N_MOD, d)),
            _resident((d, d)),
            _resident((1, d)),
            _resident((1, d)),
            _resident(w_up.shape),
            _resident(fw.shape),
            _resident((1, 2 * d_ff)),
            _resident(w_dn.shape),
            _resident((1, d)),
            _resident((1, d)),
        ],
        out_specs=_rows(tile, d),
        out_shape=jax.ShapeDtypeStruct((s_len, d), F32),
        scratch_shapes=[
            pltpu.VMEM((2 * d_ff // V7X_LANES, FFN_HIST + tile, V7X_LANES), F32),
            pltpu.VMEM((tile, d_ff), BF16),
        ],
        compiler_params=_params(),
        name="out_ffn",
    )(x2, m, mod, w_out, l1g, l1b, w_up, fw, fb, w_dn, l2g, l2b)


def kernel(x, c, positions, w_ada, b_ada, w_in, b_in, conv_dw_w, conv_dw_b, conv_ln_g, conv_ln_b,
           w_conv_out, ret_gn_g, ret_gn_b, w_ret_out, w_out, ln1_g, ln1_b, w_up, ffn_dw_w, ffn_dw_b,
           w_down, ln2_g, ln2_b):
    bsz, s_len, d = x.shape
    assert bsz == 1 and s_len % SEQ_TILE == 0 and d % V7X_LANES == 0
    depth = w_ada.shape[0]
    qk_w = 2 * d
    vg_w = 2 * w_ret_out.shape[1]
    n_lane_chunks = d // V7X_LANES

    row = lambda v: v.reshape(1, -1)
    x2 = x.reshape(s_len, d)
    pos = positions.reshape(s_len, 1).astype(F32)
    for l in range(depth):
        mod = _adaln_mod(c, w_ada[l], b_ada[l])
        w_l, b_l = w_in[l], b_in[l]
        w_main = jnp.concatenate([w_l[:, :qk_w], w_l[:, qk_w + vg_w:]], axis=1).astype(BF16)
        b_main = jnp.concatenate([b_l[:qk_w], b_l[qk_w + vg_w:]]).reshape(1, -1)
        w_vg = w_l[:, qk_w:qk_w + vg_w].reshape(d, vg_w // VG_COLS, VG_COLS).transpose(1, 0, 2).astype(BF16)
        b_vg = b_l[qk_w:qk_w + vg_w].reshape(vg_w // VG_COLS, 1, VG_COLS)
        cw3 = conv_dw_w[l].reshape(CONV_KERNEL, n_lane_chunks, V7X_LANES).transpose(1, 0, 2)
        cb3 = conv_dw_b[l].reshape(n_lane_chunks, 1, V7X_LANES)
        m = _token_branches(x2, pos, mod, w_main, b_main, w_vg, b_vg, cw3, cb3,
                            row(conv_ln_g[l]), row(conv_ln_b[l]), w_conv_out[l].astype(BF16),
                            row(ret_gn_g[l]), row(ret_gn_b[l]), w_ret_out[l].astype(BF16), SEQ_TILE)
        x2 = _out_ffn(x2, m, mod, w_out[l].astype(BF16), row(ln1_g[l]), row(ln1_b[l]), w_up[l].astype(BF16),
                      ffn_dw_w[l], row(ffn_dw_b[l]), w_down[l].astype(BF16), row(ln2_g[l]), row(ln2_b[l]),
                      SEQ_TILE)
    return x2.reshape(bsz, s_len, d)
```

```python
import functools

import jax
import jax.numpy as jnp
import numpy as np
from jax import lax
from jax.experimental import pallas as pl
from jax.experimental.pallas import tpu as pltpu

F32 = jnp.float32
BF16 = jnp.bfloat16

DEPTH = 1
N_HEADS = 8
RET_CHUNK = 128
CONV_KERNEL = 31
FFN_CONV_KERNEL = 3
ROPE_BASE = 10000.0
LN_EPS = 1e-5
DEEPNORM_ALPHA = (2.0 * DEPTH) ** 0.25
N_MOD = 6

V7X_LANES = 128
V7X_SUBLANES = 8

SEQ_TILE = 512
CONV_HIST = 32
FFN_HIST = V7X_SUBLANES
CONV_ROWS = 128
FFN_COLS = 256
VMEM_LIMIT_BYTES = 62 * 1024 * 1024


def _ln(x):
    mu = jnp.mean(x, axis=-1, keepdims=True)
    xc = x - mu
    var = jnp.mean(xc * xc, axis=-1, keepdims=True)
    return xc * lax.rsqrt(var + LN_EPS)


def _sigmoid(x):
    return 0.5 * jnp.tanh(0.5 * x) + 0.5


def _resident(shape):
    nd = len(shape)
    return pl.BlockSpec(shape, lambda i: (0,) * nd, pipeline_mode=pl.Buffered(1))


def _rows(tile, width):
    return pl.BlockSpec((tile, width), lambda i: (i, 0))


def _params():
    return pltpu.CompilerParams(dimension_semantics=("arbitrary",), vmem_limit_bytes=VMEM_LIMIT_BYTES)


def _mod_kernel(c_ref, w_ref, b_ref, o_ref):
    c = c_ref[...]
    s = c * _sigmoid(c)
    o_ref[...] = jnp.dot(s, w_ref[...], preferred_element_type=F32) + b_ref[...]


def _adaln_mod(c, w_ada, b_ada):
    d = c.shape[-1]
    c8 = jnp.broadcast_to(c, (V7X_SUBLANES, d))
    out = pl.pallas_call(
        _mod_kernel,
        grid=(N_MOD,),
        in_specs=[
            pl.BlockSpec((V7X_SUBLANES, d), lambda i: (0, 0)),
            pl.BlockSpec((d, d), lambda i: (0, i)),
            pl.BlockSpec((1, d), lambda i: (0, i)),
        ],
        out_specs=pl.BlockSpec((V7X_SUBLANES, d), lambda i: (0, i)),
        out_shape=jax.ShapeDtypeStruct((V7X_SUBLANES, N_MOD * d), F32),
        compiler_params=pltpu.CompilerParams(dimension_semantics=("arbitrary",)),
        name="adaln_mod",
    )(c8, w_ada, b_ada.reshape(1, -1))
    return out[0].reshape(N_MOD, d)


def _branches_kernel(x_ref, pos_ref, invf_ref, mod_ref, w_ref, b_ref, cw_ref, cb_ref, clg_ref, clb_ref,
                     wco_ref, gng_ref, gnb_ref, wro_ref, decay_ref, xi_ref, zeta_ref, o_ref,
                     state_ref, abuf_ref, q_ref, qx_ref, k_ref, kz_ref, v_ref, sg_ref, ract_ref,
                     *, tile, d_model, dk, dv, chunk_decay):
    n_heads = d_model // dk
    v_w = n_heads * dv
    n_lane_chunks = d_model // V7X_LANES
    n_chunks = tile // RET_CHUNK
    o_q = 0
    o_k = o_q + d_model
    o_v = o_k + d_model
    o_g = o_v + v_w
    o_cv = o_g + v_w
    o_cg = o_cv + d_model
    o_ga = o_cg + d_model
    o_gb = o_ga + d_model

    @pl.when(pl.program_id(0) == 0)
    def _():
        state_ref[...] = jnp.zeros_like(state_ref)
        abuf_ref[:, 0:CONV_HIST, :] = jnp.zeros((n_lane_chunks, CONV_HIST, V7X_LANES), F32)

    x = x_ref[...]
    h = _ln(x) * (1.0 + mod_ref[1:2, :]) + mod_ref[0:1, :]
    hb = h.astype(BF16)

    def proj(lo, width):
        return jnp.dot(hb, w_ref[:, lo:lo + width], preferred_element_type=F32) + b_ref[:, lo:lo + width]

    def anchor(v):
        bits = lax.bitcast_convert_type(v[0:1, 0:V7X_LANES].astype(F32), jnp.uint32)
        bits = lax.shift_right_logical(lax.shift_right_logical(bits, jnp.uint32(16)), jnp.uint32(16))
        return lax.bitcast_convert_type(bits, F32)

    a = proj(o_cv, d_model) * _sigmoid(proj(o_cg, d_model))
    for j in range(n_lane_chunks):
        abuf_ref[j, CONV_HIST:CONV_HIST + tile, :] = a[:, j * V7X_LANES:(j + 1) * V7X_LANES]
    first = CONV_HIST - (CONV_KERNEL - 1)

    def conv_chunk(j, zero):
        ls = slice(j * V7X_LANES, (j + 1) * V7X_LANES)
        bias = cb_ref[:, ls] if zero is None else cb_ref[:, ls] + zero
        blocks = []
        for rb in range(tile // CONV_ROWS):
            acc = jnp.broadcast_to(bias, (CONV_ROWS, V7X_LANES))
            for k in range(CONV_KERNEL):
                r0 = rb * CONV_ROWS + first + k
                acc = acc + cw_ref[k:k + 1, ls] * abuf_ref[j, r0:r0 + CONV_ROWS, :]
            blocks.append(acc)
        abuf_ref[j, 0:CONV_HIST, :] = abuf_ref[j, tile:tile + CONV_HIST, :]
        return jnp.concatenate(blocks, axis=0)

    ang = pos_ref[...] * invf_ref[...]
    cs = jnp.cos(ang)
    sn = jnp.sin(ang)
    lane = lax.broadcasted_iota(jnp.int32, ang.shape, 1)
    sn = jnp.where(lane < dk // 2, -sn, sn)
    k_scale = float(dk) ** -0.5

    def rope_heads(y, cos_t, sin_t):
        for hh in range(n_heads):
            yh = y[:, hh * dk:(hh + 1) * dk]
            yield hh, yh * cos_t + pltpu.roll(yh, dk // 2, 1) * sin_t

    def tiled(tab_ref, hh):
        return jnp.concatenate([tab_ref[:, hh * dk:(hh + 1) * dk]] * n_chunks, axis=0)

    half_v = v_w // 2
    vb0 = proj(o_v, half_v).astype(BF16)
    v_ref[:, 0:half_v] = vb0
    vb1 = proj(o_v + half_v, half_v).astype(BF16)
    v_ref[:, half_v:v_w] = vb1
    pins = {3: anchor(vb0), 6: anchor(vb1)}
    cols = [conv_chunk(j, pins.get(j)) for j in range(n_lane_chunks)]
    for hh, r in rope_heads(proj(o_q, d_model), cs, sn):
        q_ref[:, hh * dk:(hh + 1) * dk] = r.astype(BF16)
        qx_ref[:, hh * dk:(hh + 1) * dk] = (r * tiled(xi_ref, hh)).astype(BF16)
    for hh, r in rope_heads(proj(o_k, d_model), cs * k_scale, sn * k_scale):
        k_ref[:, hh * dk:(hh + 1) * dk] = r.astype(BF16)
        kz_ref[:, hh * dk:(hh + 1) * dk] = (r * tiled(zeta_ref, hh)).astype(BF16)
    g = proj(o_g, half_v)
    sg_ref[:, 0:half_v] = g * _sigmoid(g)
    g = proj(o_g + half_v, half_v)
    sg_ref[:, half_v:v_w] = g * _sigmoid(g)
    ga = _sigmoid(proj(o_ga, d_model))
    gb = _sigmoid(proj(o_gb, d_model))

    an = _ln(jnp.concatenate(cols, axis=1)) * clg_ref[...] + clb_ref[...]
    sa = (an * _sigmoid(an)).astype(BF16)
    ua = ga * jnp.dot(sa, wco_ref[...], preferred_element_type=F32)

    heads = range(n_heads)
    for ci in range(n_chunks):
        rows = slice(ci * RET_CHUNK, (ci + 1) * RET_CHUNK)
        qs = [slice(hh * dk, (hh + 1) * dk) for hh in heads]
        vs = [slice(hh * dv, (hh + 1) * dv) for hh in heads]
        sc = [lax.dot_general(q_ref[rows, qs[hh]], k_ref[rows, qs[hh]], (((1,), (1,)), ((), ())),
                              preferred_element_type=F32) for hh in heads]
        lhs = [jnp.concatenate([(sc[hh] * decay_ref[hh]).astype(BF16), qx_ref[rows, qs[hh]]], axis=1)
               for hh in heads]
        st = [state_ref[hh] for hh in heads]
        vh = [v_ref[rows, vs[hh]] for hh in heads]
        o = [jnp.dot(lhs[hh], jnp.concatenate([vh[hh], st[hh].astype(BF16)], axis=0),
                     preferred_element_type=F32) for hh in heads]
        kv = [lax.dot_general(kz_ref[rows, qs[hh]], vh[hh], (((0,), (0,)), ((), ())),
                              preferred_element_type=F32) for hh in heads]
        for hh in heads:
            state_ref[hh] = st[hh] * chunk_decay[hh] + kv[hh]
        for hh in heads:
            r = (_ln(o[hh]) * gng_ref[:, vs[hh]] + gnb_ref[:, vs[hh]]) * sg_ref[rows, vs[hh]]
            ract_ref[rows, vs[hh]] = r.astype(BF16)

    yb = jnp.dot(ract_ref[...], wro_ref[...], preferred_element_type=F32)
    o_ref[...] = (ua + gb * yb).astype(BF16)


def _token_branches(x2, pos, mod, w_in, b_in, cw, cb, clg, clb, w_co, gn_g, gn_b, w_ro, tile):
    s_len, d = x2.shape
    dk = d // N_HEADS
    dv = w_ro.shape[0] // N_HEADS
    log_gamma = np.log(1.0 - 2.0 ** (-5.0 - np.arange(N_HEADS, dtype=np.float64)))
    idx = np.arange(RET_CHUNK, dtype=np.float64)
    rel = idx[:, None] - idx[None, :]
    decay = np.where(rel[None] >= 0, np.exp(log_gamma[:, None, None] * np.maximum(rel, 0.0)[None]), 0.0)
    xi = np.repeat(np.exp(log_gamma[None, :] * (idx[:, None] + 1.0)), dk, axis=1)
    zeta = np.repeat(np.exp(log_gamma[None, :] * (RET_CHUNK - 1.0 - idx[:, None])), dk, axis=1)
    chunk_decay = tuple(float(v) for v in np.exp(log_gamma * RET_CHUNK))
    half = dk // 2
    inv_freq = ROPE_BASE ** (-jnp.arange(half, dtype=F32) / half)
    inv_freq = jnp.concatenate([inv_freq, inv_freq]).reshape(1, dk)

    kern = functools.partial(_branches_kernel, tile=tile, d_model=d, dk=dk, dv=dv, chunk_decay=chunk_decay)
    return pl.pallas_call(
        kern,
        grid=(s_len // tile,),
        in_specs=[
            _rows(tile, d),
            _rows(tile, 1),
            _resident((1, dk)),
            _resident((N_MOD, d)),
            _resident(w_in.shape),
            _resident(b_in.shape),
            _resident(cw.shape),
            _resident((1, d)),
            _resident((1, d)),
            _resident((1, d)),
            _resident((d, d)),
            _resident((1, N_HEADS * dv)),
            _resident((1, N_HEADS * dv)),
            _resident((N_HEADS * dv, d)),
            _resident((N_HEADS, RET_CHUNK, RET_CHUNK)),
            _resident((RET_CHUNK, d)),
            _resident((RET_CHUNK, d)),
        ],
        out_specs=_rows(tile, d),
        out_shape=jax.ShapeDtypeStruct((s_len, d), BF16),
        scratch_shapes=[
            pltpu.VMEM((N_HEADS, dk, dv), F32),
            pltpu.VMEM((d // V7X_LANES, CONV_HIST + tile, V7X_LANES), F32),
            pltpu.VMEM((tile, d), BF16),
            pltpu.VMEM((tile, d), BF16),
            pltpu.VMEM((tile, d), BF16),
            pltpu.VMEM((tile, d), BF16),
            pltpu.VMEM((tile, N_HEADS * dv), BF16),
            pltpu.VMEM((tile, N_HEADS * dv), F32),
            pltpu.VMEM((tile, N_HEADS * dv), BF16),
        ],
        compiler_params=_params(),
        name="token_branches",
    )(x2, pos, inv_freq, mod, w_in, b_in, cw, cb, clg, clb, w_co, gn_g, gn_b, w_ro,
      jnp.asarray(decay, F32), jnp.asarray(xi, F32), jnp.asarray(zeta, F32))


def _out_ffn_kernel(x_ref, m_ref, mod_ref, wout_ref, l1g_ref, l1b_ref, wup_ref, fw_ref, fb_ref, wdn_ref,
                    l2g_ref, l2b_ref, o_ref, ubuf_ref, act_ref, *, tile, d_ff):
    n_lane_chunks = 2 * d_ff // V7X_LANES

    @pl.when(pl.program_id(0) == 0)
    def _():
        ubuf_ref[:, 0:FFN_HIST, :] = jnp.zeros((n_lane_chunks, FFN_HIST, V7X_LANES), F32)

    t = jnp.dot(m_ref[...], wout_ref[...], preferred_element_type=F32)
    z1 = DEEPNORM_ALPHA * x_ref[...] + mod_ref[2:3, :] * t
    x1 = _ln(z1) * l1g_ref[...] + l1b_ref[...]

    h = _ln(x1) * (1.0 + mod_ref[4:5, :]) + mod_ref[3:4, :]
    hb = h.astype(BF16)

    def conv_cols(base):
        u = jnp.dot(hb, wup_ref[:, base:base + FFN_COLS], preferred_element_type=F32)
        outs = []
        for jj in range(FFN_COLS // V7X_LANES):
            j = base // V7X_LANES + jj
            ls = slice(j * V7X_LANES, (j + 1) * V7X_LANES)
            uj = u[:, jj * V7X_LANES:(jj + 1) * V7X_LANES]
            ubuf_ref[j, FFN_HIST:FFN_HIST + tile, :] = uj
            y = fb_ref[:, ls] + fw_ref[2:3, ls] * uj
            for k in range(FFN_CONV_KERNEL - 1):
                r0 = FFN_HIST - (FFN_CONV_KERNEL - 1) + k
                y = y + fw_ref[k:k + 1, ls] * ubuf_ref[j, r0:r0 + tile, :]
            ubuf_ref[j, 0:FFN_HIST, :] = ubuf_ref[j, tile:tile + FFN_HIST, :]
            outs.append(y)
        return jnp.concatenate(outs, axis=1)

    for gi in range(d_ff // FFN_COLS):
        val = conv_cols(gi * FFN_COLS)
        gate = conv_cols(d_ff + gi * FFN_COLS)
        act_ref[:, gi * FFN_COLS:(gi + 1) * FFN_COLS] = (val * (gate * _sigmoid(gate))).astype(BF16)

    f = jnp.dot(act_ref[...], wdn_ref[...], preferred_element_type=F32)
    z2 = DEEPNORM_ALPHA * x1 + mod_ref[5:6, :] * f
    o_ref[...] = _ln(z2) * l2g_ref[...] + l2b_ref[...]


def _out_ffn(x2, m, mod, w_out, l1g, l1b, w_up, fw, fb, w_dn, l2g, l2b, tile):
    s_len, d = x2.shape
    d_ff = w_dn.shape[0]
    kern = functools.partial(_out_ffn_kernel, tile=tile, d_ff=d_ff)
    return pl.pallas_call(
        kern,
        grid=(s_len // tile,),
        in_specs=[
            _rows(tile, d),
            _rows(tile, d),
            _resident((N_MOD, d)),
            _resident((d, d)),
            _resident((1, d)),
            _resident((1, d)),
            _resident(w_up.shape),
            _resident(fw.shape),
            _resident((1, 2 * d_ff)),
            _resident(w_dn.shape),
            _resident((1, d)),
            _resident((1, d)),
        ],
        out_specs=_rows(tile, d),
        out_shape=jax.ShapeDtypeStruct((s_len, d), F32),
        scratch_shapes=[
            pltpu.VMEM((2 * d_ff // V7X_LANES, FFN_HIST + tile, V7X_LANES), F32),
            pltpu.VMEM((tile, d_ff), BF16),
        ],
        compiler_params=_params(),
        name="out_ffn",
    )(x2, m, mod, w_out, l1g, l1b, w_up, fw, fb, w_dn, l2g, l2b)


def kernel(x, c, positions, w_ada, b_ada, w_in, b_in, conv_dw_w, conv_dw_b, conv_ln_g, conv_ln_b,
           w_conv_out, ret_gn_g, ret_gn_b, w_ret_out, w_out, ln1_g, ln1_b, w_up, ffn_dw_w, ffn_dw_b,
           w_down, ln2_g, ln2_b):
    bsz, s_len, d = x.shape
    assert bsz == 1 and s_len % SEQ_TILE == 0 and d % V7X_LANES == 0
    depth = w_ada.shape[0]

    row = lambda v: v.reshape(1, -1)
    x2 = x.reshape(s_len, d)
    pos = positions.reshape(s_len, 1).astype(F32)
    for l in range(depth):
        mod = _adaln_mod(c, w_ada[l], b_ada[l])
        m = _token_branches(x2, pos, mod, w_in[l].astype(BF16), row(b_in[l]), conv_dw_w[l], row(conv_dw_b[l]),
                            row(conv_ln_g[l]), row(conv_ln_b[l]), w_conv_out[l].astype(BF16),
                            row(ret_gn_g[l]), row(ret_gn_b[l]), w_ret_out[l].astype(BF16), SEQ_TILE)
        x2 = _out_ffn(x2, m, mod, w_out[l].astype(BF16), row(ln1_g[l]), row(ln1_b[l]), w_up[l].astype(BF16),
                      ffn_dw_w[l], row(ffn_dw_b[l]), w_down[l].astype(BF16), row(ln2_g[l]), row(ln2_b[l]),
                      SEQ_TILE)
    return x2.reshape(bsz, s_len, d)
```

```python
import functools

import jax
import jax.numpy as jnp
import numpy as np
from jax import lax
from jax.experimental import pallas as pl
from jax.experimental.pallas import tpu as pltpu

F32 = jnp.float32
BF16 = jnp.bfloat16

DEPTH = 1
N_HEADS = 8
RET_CHUNK = 128
CONV_KERNEL = 31
FFN_CONV_KERNEL = 3
ROPE_BASE = 10000.0
LN_EPS = 1e-5
DEEPNORM_ALPHA = (2.0 * DEPTH) ** 0.25
N_MOD = 6

V7X_LANES = 128
V7X_SUBLANES = 8

SEQ_TILE = 512
CONV_HIST = 32
FFN_HIST = V7X_SUBLANES
CONV_ROWS = 128
FFN_COLS = 256
VMEM_LIMIT_BYTES = 62 * 1024 * 1024
STAGE_CHUNK_BYTES = 512 * 1024


def _ln(x):
    mu = jnp.mean(x, axis=-1, keepdims=True)
    xc = x - mu
    var = jnp.mean(xc * xc, axis=-1, keepdims=True)
    return xc * lax.rsqrt(var + LN_EPS)


def _sigmoid(x):
    return 0.5 * jnp.tanh(0.5 * x) + 0.5


def _resident(shape):
    nd = len(shape)
    return pl.BlockSpec(shape, lambda i: (0,) * nd, pipeline_mode=pl.Buffered(1))


def _rows(tile, width):
    return pl.BlockSpec((tile, width), lambda i: (i, 0))


def _params():
    return pltpu.CompilerParams(dimension_semantics=("arbitrary",), vmem_limit_bytes=VMEM_LIMIT_BYTES)


_HBM = pl.BlockSpec(memory_space=pl.ANY)


def _stage_rows(width):
    rows = max(STAGE_CHUNK_BYTES // (width * 4), 2 * V7X_SUBLANES)
    return rows - rows % (2 * V7X_SUBLANES)


def _stage_scratch(width):
    return [pltpu.VMEM((2, _stage_rows(width), width), F32), pltpu.SemaphoreType.DMA((2,))]


def _load_weight_bf16(w_hbm, w_vmem, stage, sem):
    rows = stage.shape[1]
    n_chunks = w_hbm.shape[0] // rows
    assert n_chunks * rows == w_hbm.shape[0] and stage.shape[2] == w_hbm.shape[1]

    def copy(c, slot):
        r0 = pl.multiple_of(c * rows, rows)
        return pltpu.make_async_copy(w_hbm.at[pl.ds(r0, rows), :], stage.at[slot], sem.at[slot])

    copy(0, 0).start()

    def body(c, carry):
        slot = c % 2

        @pl.when(c + 1 < n_chunks)
        def _():
            copy(c + 1, 1 - slot).start()

        copy(c, slot).wait()
        r0 = pl.multiple_of(c * rows, rows)
        w_vmem[pl.ds(r0, rows), :] = stage[slot].astype(BF16)
        return carry

    lax.fori_loop(0, n_chunks, body, 0)


def _mod_kernel(c_ref, w_ref, b_ref, o_ref):
    c = c_ref[...]
    s = c * _sigmoid(c)
    o_ref[...] = jnp.dot(s, w_ref[...], preferred_element_type=F32) + b_ref[...]


def _adaln_mod(c, w_ada, b_ada):
    d = c.shape[-1]
    c8 = jnp.broadcast_to(c, (V7X_SUBLANES, d))
    out = pl.pallas_call(
        _mod_kernel,
        grid=(N_MOD,),
        in_specs=[
            pl.BlockSpec((V7X_SUBLANES, d), lambda i: (0, 0)),
            pl.BlockSpec((d, d), lambda i: (0, i)),
            pl.BlockSpec((1, d), lambda i: (0, i)),
        ],
        out_specs=pl.BlockSpec((V7X_SUBLANES, d), lambda i: (0, i)),
        out_shape=jax.ShapeDtypeStruct((V7X_SUBLANES, N_MOD * d), F32),
        compiler_params=pltpu.CompilerParams(dimension_semantics=("arbitrary",)),
        name="adaln_mod",
    )(c8, w_ada, b_ada.reshape(1, -1))
    return out[0].reshape(N_MOD, d)


def _branches_kernel(x_ref, pos_ref, invf_ref, mod_ref, w_hbm, b_ref, cw_ref, cb_ref, clg_ref, clb_ref,
                     wco_hbm, gng_ref, gnb_ref, wro_hbm, decay_ref, xi_ref, zeta_ref, o_ref,
                     state_ref, abuf_ref, q_ref, qx_ref, k_ref, kz_ref, v_ref, sg_ref, ract_ref,
                     w_ref, wco_ref, wro_ref, stage_in, sem_in, stage_d, sem_d,
                     *, tile, d_model, dk, dv, chunk_decay):
    n_heads = d_model // dk
    v_w = n_heads * dv
    n_lane_chunks = d_model // V7X_LANES
    n_chunks = tile // RET_CHUNK
    o_q = 0
    o_k = o_q + d_model
    o_v = o_k + d_model
    o_g = o_v + v_w
    o_cv = o_g + v_w
    o_cg = o_cv + d_model
    o_ga = o_cg + d_model
    o_gb = o_ga + d_model

    @pl.when(pl.program_id(0) == 0)
    def _():
        state_ref[...] = jnp.zeros_like(state_ref)
        abuf_ref[:, 0:CONV_HIST, :] = jnp.zeros((n_lane_chunks, CONV_HIST, V7X_LANES), F32)
        _load_weight_bf16(w_hbm, w_ref, stage_in, sem_in)
        _load_weight_bf16(wco_hbm, wco_ref, stage_d, sem_d)
        _load_weight_bf16(wro_hbm, wro_ref, stage_d, sem_d)

    x = x_ref[...]
    h = _ln(x) * (1.0 + mod_ref[1:2, :]) + mod_ref[0:1, :]
    hb = h.astype(BF16)

    def proj(lo, width):
        return jnp.dot(hb, w_ref[:, lo:lo + width], preferred_element_type=F32) + b_ref[:, lo:lo + width]

    def anchor(v):
        bits = lax.bitcast_convert_type(v[0:1, 0:V7X_LANES].astype(F32), jnp.uint32)
        bits = lax.shift_right_logical(lax.shift_right_logical(bits, jnp.uint32(16)), jnp.uint32(16))
        return lax.bitcast_convert_type(bits, F32)

    a = proj(o_cv, d_model) * _sigmoid(proj(o_cg, d_model))
    for j in range(n_lane_chunks):
        abuf_ref[j, CONV_HIST:CONV_HIST + tile, :] = a[:, j * V7X_LANES:(j + 1) * V7X_LANES]
    first = CONV_HIST - (CONV_KERNEL - 1)

    def conv_chunk(j, zero):
        ls = slice(j * V7X_LANES, (j + 1) * V7X_LANES)
        bias = cb_ref[:, ls] if zero is None else cb_ref[:, ls] + zero
        blocks = []
        for rb in range(tile // CONV_ROWS):
            acc = jnp.broadcast_to(bias, (CONV_ROWS, V7X_LANES))
            for k in range(CONV_KERNEL):
                r0 = rb * CONV_ROWS + first + k
                acc = acc + cw_ref[k:k + 1, ls] * abuf_ref[j, r0:r0 + CONV_ROWS, :]
            blocks.append(acc)
        abuf_ref[j, 0:CONV_HIST, :] = abuf_ref[j, tile:tile + CONV_HIST, :]
        return jnp.concatenate(blocks, axis=0)

    ang = pos_ref[...] * invf_ref[...]
    cs = jnp.cos(ang)
    sn = jnp.sin(ang)
    lane = lax.broadcasted_iota(jnp.int32, ang.shape, 1)
    sn = jnp.where(lane < dk // 2, -sn, sn)
    k_scale = float(dk) ** -0.5

    def rope_heads(y, cos_t, sin_t):
        for hh in range(n_heads):
            yh = y[:, hh * dk:(hh + 1) * dk]
            yield hh, yh * cos_t + pltpu.roll(yh, dk // 2, 1) * sin_t

    def tiled(tab_ref, hh):
        return jnp.concatenate([tab_ref[:, hh * dk:(hh + 1) * dk]] * n_chunks, axis=0)

    half_v = v_w // 2
    vb0 = proj(o_v, half_v).astype(BF16)
    v_ref[:, 0:half_v] = vb0
    vb1 = proj(o_v + half_v, half_v).astype(BF16)
    v_ref[:, half_v:v_w] = vb1
    pins = {3: anchor(vb0), 6: anchor(vb1)}
    cols = [conv_chunk(j, pins.get(j)) for j in range(n_lane_chunks)]
    for hh, r in rope_heads(proj(o_q, d_model), cs, sn):
        q_ref[:, hh * dk:(hh + 1) * dk] = r.astype(BF16)
        qx_ref[:, hh * dk:(hh + 1) * dk] = (r * tiled(xi_ref, hh)).astype(BF16)
    for hh, r in rope_heads(proj(o_k, d_model), cs * k_scale, sn * k_scale):
        k_ref[:, hh * dk:(hh + 1) * dk] = r.astype(BF16)
        kz_ref[:, hh * dk:(hh + 1) * dk] = (r * tiled(zeta_ref, hh)).astype(BF16)
    g = proj(o_g, half_v)
    sg_ref[:, 0:half_v] = g * _sigmoid(g)
    g = proj(o_g + half_v, half_v)
    sg_ref[:, half_v:v_w] = g * _sigmoid(g)
    ga = _sigmoid(proj(o_ga, d_model))
    gb = _sigmoid(proj(o_gb, d_model))

    an = _ln(jnp.concatenate(cols, axis=1)) * clg_ref[...] + clb_ref[...]
    sa = (an * _sigmoid(an)).astype(BF16)
    ua = ga * jnp.dot(sa, wco_ref[...], preferred_element_type=F32)

    heads = range(n_heads)
    for ci in range(n_chunks):
        rows = slice(ci * RET_CHUNK, (ci + 1) * RET_CHUNK)
        qs = [slice(hh * dk, (hh + 1) * dk) for hh in heads]
        vs = [slice(hh * dv, (hh + 1) * dv) for hh in heads]
        sc = [lax.dot_general(q_ref[rows, qs[hh]], k_ref[rows, qs[hh]], (((1,), (1,)), ((), ())),
                              preferred_element_type=F32) for hh in heads]
        lhs = [jnp.concatenate([(sc[hh] * decay_ref[hh]).astype(BF16), qx_ref[rows, qs[hh]]], axis=1)
               for hh in heads]
        st = [state_ref[hh] for hh in heads]
        vh = [v_ref[rows, vs[hh]] for hh in heads]
        o = [jnp.dot(lhs[hh], jnp.concatenate([vh[hh], st[hh].astype(BF16)], axis=0),
                     preferred_element_type=F32) for hh in heads]
        kv = [lax.dot_general(kz_ref[rows, qs[hh]], vh[hh], (((0,), (0,)), ((), ())),
                              preferred_element_type=F32) for hh in heads]
        for hh in heads:
            state_ref[hh] = st[hh] * chunk_decay[hh] + kv[hh]
        for hh in heads:
            r = (_ln(o[hh]) * gng_ref[:, vs[hh]] + gnb_ref[:, vs[hh]]) * sg_ref[rows, vs[hh]]
            ract_ref[rows, vs[hh]] = r.astype(BF16)

    yb = jnp.dot(ract_ref[...], wro_ref[...], preferred_element_type=F32)
    o_ref[...] = (ua + gb * yb).astype(BF16)


def _token_branches(x2, pos, mod, w_in, b_in, cw, cb, clg, clb, w_co, gn_g, gn_b, w_ro, tile):
    s_len, d = x2.shape
    dk = d // N_HEADS
    dv = w_ro.shape[0] // N_HEADS
    log_gamma = np.log(1.0 - 2.0 ** (-5.0 - np.arange(N_HEADS, dtype=np.float64)))
    idx = np.arange(RET_CHUNK, dtype=np.float64)
    rel = idx[:, None] - idx[None, :]
    decay = np.where(rel[None] >= 0, np.exp(log_gamma[:, None, None] * np.maximum(rel, 0.0)[None]), 0.0)
    xi = np.repeat(np.exp(log_gamma[None, :] * (idx[:, None] + 1.0)), dk, axis=1)
    zeta = np.repeat(np.exp(log_gamma[None, :] * (RET_CHUNK - 1.0 - idx[:, None])), dk, axis=1)
    chunk_decay = tuple(float(v) for v in np.exp(log_gamma * RET_CHUNK))
    half = dk // 2
    inv_freq = ROPE_BASE ** (-jnp.arange(half, dtype=F32) / half)
    inv_freq = jnp.concatenate([inv_freq, inv_freq]).reshape(1, dk)

    kern = functools.partial(_branches_kernel, tile=tile, d_model=d, dk=dk, dv=dv, chunk_decay=chunk_decay)
    return pl.pallas_call(
        kern,
        grid=(s_len // tile,),
        in_specs=[
            _rows(tile, d),
            _rows(tile, 1),
            _resident((1, dk)),
            _resident((N_MOD, d)),
            _HBM,
            _resident(b_in.shape),
            _resident(cw.shape),
            _resident((1, d)),
            _resident((1, d)),
            _resident((1, d)),
            _HBM,
            _resident((1, N_HEADS * dv)),
            _resident((1, N_HEADS * dv)),
            _HBM,
            _resident((N_HEADS, RET_CHUNK, RET_CHUNK)),
            _resident((RET_CHUNK, d)),
            _resident((RET_CHUNK, d)),
        ],
        out_specs=_rows(tile, d),
        out_shape=jax.ShapeDtypeStruct((s_len, d), BF16),
        scratch_shapes=[
            pltpu.VMEM((N_HEADS, dk, dv), F32),
            pltpu.VMEM((d // V7X_LANES, CONV_HIST + tile, V7X_LANES), F32),
            pltpu.VMEM((tile, d), BF16),
            pltpu.VMEM((tile, d), BF16),
            pltpu.VMEM((tile, d), BF16),
            pltpu.VMEM((tile, d), BF16),
            pltpu.VMEM((tile, N_HEADS * dv), BF16),
            pltpu.VMEM((tile, N_HEADS * dv), F32),
            pltpu.VMEM((tile, N_HEADS * dv), BF16),
            pltpu.VMEM(w_in.shape, BF16),
            pltpu.VMEM(w_co.shape, BF16),
            pltpu.VMEM(w_ro.shape, BF16),
            *_stage_scratch(w_in.shape[1]),
            *_stage_scratch(d),
        ],
        compiler_params=_params(),
        name="token_branches",
    )(x2, pos, inv_freq, mod, w_in, b_in, cw, cb, clg, clb, w_co, gn_g, gn_b, w_ro,
      jnp.asarray(decay, F32), jnp.asarray(xi, F32), jnp.asarray(zeta, F32))


def _out_ffn_kernel(x_ref, m_ref, mod_ref, wout_hbm, l1g_ref, l1b_ref, wup_hbm, fw_ref, fb_ref, wdn_hbm,
                    l2g_ref, l2b_ref, o_ref, ubuf_ref, act_ref,
                    wout_ref, wup_ref, wdn_ref, stage_up, sem_up, stage_d, sem_d, *, tile, d_ff):
    n_lane_chunks = 2 * d_ff // V7X_LANES

    @pl.when(pl.program_id(0) == 0)
    def _():
        ubuf_ref[:, 0:FFN_HIST, :] = jnp.zeros((n_lane_chunks, FFN_HIST, V7X_LANES), F32)
        _load_weight_bf16(wout_hbm, wout_ref, stage_d, sem_d)
        _load_weight_bf16(wup_hbm, wup_ref, stage_up, sem_up)
        _load_weight_bf16(wdn_hbm, wdn_ref, stage_d, sem_d)

    t = jnp.dot(m_ref[...], wout_ref[...], preferred_element_type=F32)
    z1 = DEEPNORM_ALPHA * x_ref[...] + mod_ref[2:3, :] * t
    x1 = _ln(z1) * l1g_ref[...] + l1b_ref[...]

    h = _ln(x1) * (1.0 + mod_ref[4:5, :]) + mod_ref[3:4, :]
    hb = h.astype(BF16)

    def conv_cols(base):
        u = jnp.dot(hb, wup_ref[:, base:base + FFN_COLS], preferred_element_type=F32)
        outs = []
        for jj in range(FFN_COLS // V7X_LANES):
            j = base // V7X_LANES + jj
            ls = slice(j * V7X_LANES, (j + 1) * V7X_LANES)
            uj = u[:, jj * V7X_LANES:(jj + 1) * V7X_LANES]
            ubuf_ref[j, FFN_HIST:FFN_HIST + tile, :] = uj
            y = fb_ref[:, ls] + fw_ref[2:3, ls] * uj
            for k in range(FFN_CONV_KERNEL - 1):
                r0 = FFN_HIST - (FFN_CONV_KERNEL - 1) + k
                y = y + fw_ref[k:k + 1, ls] * ubuf_ref[j, r0:r0 + tile, :]
            ubuf_ref[j, 0:FFN_HIST, :] = ubuf_ref[j, tile:tile + FFN_HIST, :]
            outs.append(y)
        return jnp.concatenate(outs, axis=1)

    for gi in range(d_ff // FFN_COLS):
        val = conv_cols(gi * FFN_COLS)
        gate = conv_cols(d_ff + gi * FFN_COLS)
        act_ref[:, gi * FFN_COLS:(gi + 1) * FFN_COLS] = (val * (gate * _sigmoid(gate))).astype(BF16)

    f = jnp.dot(act_ref[...], wdn_ref[...], preferred_element_type=F32)
    z2 = DEEPNORM_ALPHA * x1 + mod_ref[5:6, :] * f
    o_ref[...] = _ln(z2) * l2g_ref[...] + l2b_ref[...]


def _out_ffn(x2, m, mod, w_out, l1g, l1b, w_up, fw, fb, w_dn, l2g, l2b, tile):
    s_len, d = x2.shape
    d_ff = w_dn.shape[0]
    kern = functools.partial(_out_ffn_kernel, tile=tile, d_ff=d_ff)
    return pl.pallas_call(
        kern,
        grid=(s_len // tile,),
        in_specs=[
            _rows(tile, d),
            _rows(tile, d),
            _resident((N_MOD, d)),
            _HBM,
            _resident((1, d)),
            _resident((1, d)),
            _HBM,
            _resident(fw.shape),
            _resident((1, 2 * d_ff)),
            _HBM,
            _resident((1, d)),
            _resident((1, d)),
        ],
        out_specs=_rows(tile, d),
        out_shape=jax.ShapeDtypeStruct((s_len, d), F32),
        scratch_shapes=[
            pltpu.VMEM((2 * d_ff // V7X_LANES, FFN_HIST + tile, V7X_LANES), F32),
            pltpu.VMEM((tile, d_ff), BF16),
            pltpu.VMEM(w_out.shape, BF16),
            pltpu.VMEM(w_up.shape, BF16),
            pltpu.VMEM(w_dn.shape, BF16),
            *_stage_scratch(w_up.shape[1]),
            *_stage_scratch(d),
        ],
        compiler_params=_params(),
        name="out_ffn",
    )(x2, m, mod, w_out, l1g, l1b, w_up, fw, fb, w_dn, l2g, l2b)


def kernel(x, c, positions, w_ada, b_ada, w_in, b_in, conv_dw_w, conv_dw_b, conv_ln_g, conv_ln_b,
           w_conv_out, ret_gn_g, ret_gn_b, w_ret_out, w_out, ln1_g, ln1_b, w_up, ffn_dw_w, ffn_dw_b,
           w_down, ln2_g, ln2_b):
    bsz, s_len, d = x.shape
    assert bsz == 1 and s_len % SEQ_TILE == 0 and d % V7X_LANES == 0
    depth = w_ada.shape[0]

    row = lambda v: v.reshape(1, -1)
    x2 = x.reshape(s_len, d)
    pos = positions.reshape(s_len, 1).astype(F32)
    for l in range(depth):
        mod = _adaln_mod(c, w_ada[l], b_ada[l])
        m = _token_branches(x2, pos, mod, w_in[l], row(b_in[l]), conv_dw_w[l], row(conv_dw_b[l]),
                            row(conv_ln_g[l]), row(conv_ln_b[l]), w_conv_out[l],
                            row(ret_gn_g[l]), row(ret_gn_b[l]), w_ret_out[l], SEQ_TILE)
        x2 = _out_ffn(x2, m, mod, w_out[l], row(ln1_g[l]), row(ln1_b[l]), w_up[l],
                      ffn_dw_w[l], row(ffn_dw_b[l]), w_down[l], row(ln2_g[l]), row(ln2_b[l]),
                      SEQ_TILE)
    return x2.reshape(bsz, s_len, d)
```

```python
import functools

import jax
import jax.numpy as jnp
import numpy as np
from jax import lax
from jax.experimental import pallas as pl
from jax.experimental.pallas import tpu as pltpu

F32 = jnp.float32
BF16 = jnp.bfloat16

DEPTH = 1
N_HEADS = 8
RET_CHUNK = 128
CONV_KERNEL = 31
FFN_CONV_KERNEL = 3
ROPE_BASE = 10000.0
LN_EPS = 1e-5
DEEPNORM_ALPHA = (2.0 * DEPTH) ** 0.25
N_MOD = 6

V7X_LANES = 128
V7X_SUBLANES = 8

SEQ_TILE = 512
CONV_HIST = 32
FFN_HIST = V7X_SUBLANES
CONV_ROWS = 128
FFN_COLS = 256
VMEM_LIMIT_BYTES = 62 * 1024 * 1024
STAGE_CHUNK_BYTES = 256 * 1024
STAGE_SLOTS = 4


def _ln(x):
    mu = jnp.mean(x, axis=-1, keepdims=True)
    xc = x - mu
    var = jnp.mean(xc * xc, axis=-1, keepdims=True)
    return xc * lax.rsqrt(var + LN_EPS)


def _sigmoid(x):
    return 0.5 * jnp.tanh(0.5 * x) + 0.5


def _resident(shape):
    nd = len(shape)
    return pl.BlockSpec(shape, lambda i: (0,) * nd, pipeline_mode=pl.Buffered(1))


def _rows(tile, width):
    return pl.BlockSpec((tile, width), lambda i: (i, 0))


def _params():
    return pltpu.CompilerParams(dimension_semantics=("arbitrary",), vmem_limit_bytes=VMEM_LIMIT_BYTES)


_HBM = pl.BlockSpec(memory_space=pl.ANY)


def _stage_rows(width):
    rows = max(STAGE_CHUNK_BYTES // (width * 4), 2 * V7X_SUBLANES)
    return rows - rows % (2 * V7X_SUBLANES)


def _stage_scratch(width):
    return [pltpu.VMEM((STAGE_SLOTS, _stage_rows(width), width), F32), pltpu.SemaphoreType.DMA((STAGE_SLOTS,))]


def _load_weight_bf16(w_hbm, w_vmem, stage, sem):
    n_slots, rows = stage.shape[0], stage.shape[1]
    n_chunks = w_hbm.shape[0] // rows
    assert n_chunks * rows == w_hbm.shape[0] and stage.shape[2] == w_hbm.shape[1] and n_chunks >= n_slots

    def copy(c, slot):
        r0 = pl.multiple_of(c * rows, rows)
        return pltpu.make_async_copy(w_hbm.at[pl.ds(r0, rows), :], stage.at[slot], sem.at[slot])

    for c in range(n_slots - 1):
        copy(c, c).start()

    def body(c, carry):
        slot = c % n_slots
        ahead = c + n_slots - 1

        @pl.when(ahead < n_chunks)
        def _():
            copy(ahead, ahead % n_slots).start()

        copy(c, slot).wait()
        r0 = pl.multiple_of(c * rows, rows)
        w_vmem[pl.ds(r0, rows), :] = stage[slot].astype(BF16)
        return carry

    lax.fori_loop(0, n_chunks, body, 0)


def _mod_kernel(c_ref, w_ref, b_ref, o_ref):
    c = c_ref[...]
    s = c * _sigmoid(c)
    o_ref[...] = jnp.dot(s, w_ref[...], preferred_element_type=F32) + b_ref[...]


def _adaln_mod(c, w_ada, b_ada):
    d = c.shape[-1]
    c8 = jnp.broadcast_to(c, (V7X_SUBLANES, d))
    out = pl.pallas_call(
        _mod_kernel,
        grid=(N_MOD,),
        in_specs=[
            pl.BlockSpec((V7X_SUBLANES, d), lambda i: (0, 0)),
            pl.BlockSpec((d, d), lambda i: (0, i)),
            pl.BlockSpec((1, d), lambda i: (0, i)),
        ],
        out_specs=pl.BlockSpec((V7X_SUBLANES, d), lambda i: (0, i)),
        out_shape=jax.ShapeDtypeStruct((V7X_SUBLANES, N_MOD * d), F32),
        compiler_params=pltpu.CompilerParams(dimension_semantics=("arbitrary",)),
        name="adaln_mod",
    )(c8, w_ada, b_ada.reshape(1, -1))
    return out[0].reshape(N_MOD, d)


def _branches_kernel(x_ref, pos_ref, invf_ref, mod_ref, w_hbm, b_ref, cw_ref, cb_ref, clg_ref, clb_ref,
                     wco_hbm, gng_ref, gnb_ref, wro_hbm, decay_ref, xi_ref, zeta_ref, o_ref,
                     state_ref, abuf_ref, q_ref, qx_ref, k_ref, kz_ref, v_ref, sg_ref, ract_ref,
                     w_ref, wco_ref, wro_ref, stage_in, sem_in, stage_d, sem_d,
                     *, tile, d_model, dk, dv, chunk_decay):
    n_heads = d_model // dk
    v_w = n_heads * dv
    n_lane_chunks = d_model // V7X_LANES
    n_chunks = tile // RET_CHUNK
    o_q = 0
    o_k = o_q + d_model
    o_v = o_k + d_model
    o_g = o_v + v_w
    o_cv = o_g + v_w
    o_cg = o_cv + d_model
    o_ga = o_cg + d_model
    o_gb = o_ga + d_model

    @pl.when(pl.program_id(0) == 0)
    def _():
        state_ref[...] = jnp.zeros_like(state_ref)
        abuf_ref[:, 0:CONV_HIST, :] = jnp.zeros((n_lane_chunks, CONV_HIST, V7X_LANES), F32)
        _load_weight_bf16(w_hbm, w_ref, stage_in, sem_in)
        _load_weight_bf16(wco_hbm, wco_ref, stage_d, sem_d)
        _load_weight_bf16(wro_hbm, wro_ref, stage_d, sem_d)

    x = x_ref[...]
    h = _ln(x) * (1.0 + mod_ref[1:2, :]) + mod_ref[0:1, :]
    hb = h.astype(BF16)

    def proj(lo, width):
        return jnp.dot(hb, w_ref[:, lo:lo + width], preferred_element_type=F32) + b_ref[:, lo:lo + width]

    def anchor(v):
        bits = lax.bitcast_convert_type(v[0:1, 0:V7X_LANES].astype(F32), jnp.uint32)
        bits = lax.shift_right_logical(lax.shift_right_logical(bits, jnp.uint32(16)), jnp.uint32(16))
        return lax.bitcast_convert_type(bits, F32)

    a = proj(o_cv, d_model) * _sigmoid(proj(o_cg, d_model))
    for j in range(n_lane_chunks):
        abuf_ref[j, CONV_HIST:CONV_HIST + tile, :] = a[:, j * V7X_LANES:(j + 1) * V7X_LANES]
    first = CONV_HIST - (CONV_KERNEL - 1)

    def conv_chunk(j, zero):
        ls = slice(j * V7X_LANES, (j + 1) * V7X_LANES)
        bias = cb_ref[:, ls] if zero is None else cb_ref[:, ls] + zero
        blocks = []
        for rb in range(tile // CONV_ROWS):
            acc = jnp.broadcast_to(bias, (CONV_ROWS, V7X_LANES))
            for k in range(CONV_KERNEL):
                r0 = rb * CONV_ROWS + first + k
                acc = acc + cw_ref[k:k + 1, ls] * abuf_ref[j, r0:r0 + CONV_ROWS, :]
            blocks.append(acc)
        abuf_ref[j, 0:CONV_HIST, :] = abuf_ref[j, tile:tile + CONV_HIST, :]
        return jnp.concatenate(blocks, axis=0)

    ang = pos_ref[...] * invf_ref[...]
    cs = jnp.cos(ang)
    sn = jnp.sin(ang)
    lane = lax.broadcasted_iota(jnp.int32, ang.shape, 1)
    sn = jnp.where(lane < dk // 2, -sn, sn)
    k_scale = float(dk) ** -0.5

    def rope_heads(y, cos_t, sin_t):
        for hh in range(n_heads):
            yh = y[:, hh * dk:(hh + 1) * dk]
            yield hh, yh * cos_t + pltpu.roll(yh, dk // 2, 1) * sin_t

    def tiled(tab_ref, hh):
        return jnp.concatenate([tab_ref[:, hh * dk:(hh + 1) * dk]] * n_chunks, axis=0)

    half_v = v_w // 2
    vb0 = proj(o_v, half_v).astype(BF16)
    v_ref[:, 0:half_v] = vb0
    vb1 = proj(o_v + half_v, half_v).astype(BF16)
    v_ref[:, half_v:v_w] = vb1
    pins = {3: anchor(vb0), 6: anchor(vb1)}
    cols = [conv_chunk(j, pins.get(j)) for j in range(n_lane_chunks)]
    for hh, r in rope_heads(proj(o_q, d_model), cs, sn):
        q_ref[:, hh * dk:(hh + 1) * dk] = r.astype(BF16)
        qx_ref[:, hh * dk:(hh + 1) * dk] = (r * tiled(xi_ref, hh)).astype(BF16)
    for hh, r in rope_heads(proj(o_k, d_model), cs * k_scale, sn * k_scale):
        k_ref[:, hh * dk:(hh + 1) * dk] = r.astype(BF16)
        kz_ref[:, hh * dk:(hh + 1) * dk] = (r * tiled(zeta_ref, hh)).astype(BF16)
    g = proj(o_g, half_v)
    sg_ref[:, 0:half_v] = g * _sigmoid(g)
    g = proj(o_g + half_v, half_v)
    sg_ref[:, half_v:v_w] = g * _sigmoid(g)
    ga = _sigmoid(proj(o_ga, d_model))
    gb = _sigmoid(proj(o_gb, d_model))

    an = _ln(jnp.concatenate(cols, axis=1)) * clg_ref[...] + clb_ref[...]
    sa = (an * _sigmoid(an)).astype(BF16)
    ua = ga * jnp.dot(sa, wco_ref[...], preferred_element_type=F32)

    heads = range(n_heads)
    for ci in range(n_chunks):
        rows = slice(ci * RET_CHUNK, (ci + 1) * RET_CHUNK)
        qs = [slice(hh * dk, (hh + 1) * dk) for hh in heads]
        vs = [slice(hh * dv, (hh + 1) * dv) for hh in heads]
        sc = [lax.dot_general(q_ref[rows, qs[hh]], k_ref[rows, qs[hh]], (((1,), (1,)), ((), ())),
                              preferred_element_type=F32) for hh in heads]
        lhs = [jnp.concatenate([(sc[hh] * decay_ref[hh]).astype(BF16), qx_ref[rows, qs[hh]]], axis=1)
               for hh in heads]
        st = [state_ref[hh] for hh in heads]
        vh = [v_ref[rows, vs[hh]] for hh in heads]
        o = [jnp.dot(lhs[hh], jnp.concatenate([vh[hh], st[hh].astype(BF16)], axis=0),
                     preferred_element_type=F32) for hh in heads]
        kv = [lax.dot_general(kz_ref[rows, qs[hh]], vh[hh], (((0,), (0,)), ((), ())),
                              preferred_element_type=F32) for hh in heads]
        for hh in heads:
            state_ref[hh] = st[hh] * chunk_decay[hh] + kv[hh]
        for hh in heads:
            r = (_ln(o[hh]) * gng_ref[:, vs[hh]] + gnb_ref[:, vs[hh]]) * sg_ref[rows, vs[hh]]
            ract_ref[rows, vs[hh]] = r.astype(BF16)

    yb = jnp.dot(ract_ref[...], wro_ref[...], preferred_element_type=F32)
    o_ref[...] = (ua + gb * yb).astype(BF16)


def _token_branches(x2, pos, mod, w_in, b_in, cw, cb, clg, clb, w_co, gn_g, gn_b, w_ro, tile):
    s_len, d = x2.shape
    dk = d // N_HEADS
    dv = w_ro.shape[0] // N_HEADS
    log_gamma = np.log(1.0 - 2.0 ** (-5.0 - np.arange(N_HEADS, dtype=np.float64)))
    idx = np.arange(RET_CHUNK, dtype=np.float64)
    rel = idx[:, None] - idx[None, :]
    decay = np.where(rel[None] >= 0, np.exp(log_gamma[:, None, None] * np.maximum(rel, 0.0)[None]), 0.0)
    xi = np.repeat(np.exp(log_gamma[None, :] * (idx[:, None] + 1.0)), dk, axis=1)
    zeta = np.repeat(np.exp(log_gamma[None, :] * (RET_CHUNK - 1.0 - idx[:, None])), dk, axis=1)
    chunk_decay = tuple(float(v) for v in np.exp(log_gamma * RET_CHUNK))
    half = dk // 2
    inv_freq = ROPE_BASE ** (-jnp.arange(half, dtype=F32) / half)
    inv_freq = jnp.concatenate([inv_freq, inv_freq]).reshape(1, dk)

    kern = functools.partial(_branches_kernel, tile=tile, d_model=d, dk=dk, dv=dv, chunk_decay=chunk_decay)
    return pl.pallas_call(
        kern,
        grid=(s_len // tile,),
        in_specs=[
            _rows(tile, d),
            _rows(tile, 1),
            _resident((1, dk)),
            _resident((N_MOD, d)),
            _HBM,
            _resident(b_in.shape),
            _resident(cw.shape),
            _resident((1, d)),
            _resident((1, d)),
            _resident((1, d)),
            _HBM,
            _resident((1, N_HEADS * dv)),
            _resident((1, N_HEADS * dv)),
            _HBM,
            _resident((N_HEADS, RET_CHUNK, RET_CHUNK)),
            _resident((RET_CHUNK, d)),
            _resident((RET_CHUNK, d)),
        ],
        out_specs=_rows(tile, d),
        out_shape=jax.ShapeDtypeStruct((s_len, d), BF16),
        scratch_shapes=[
            pltpu.VMEM((N_HEADS, dk, dv), F32),
            pltpu.VMEM((d // V7X_LANES, CONV_HIST + tile, V7X_LANES), F32),
            pltpu.VMEM((tile, d), BF16),
            pltpu.VMEM((tile, d), BF16),
            pltpu.VMEM((tile, d), BF16),
            pltpu.VMEM((tile, d), BF16),
            pltpu.VMEM((tile, N_HEADS * dv), BF16),
            pltpu.VMEM((tile, N_HEADS * dv), F32),
            pltpu.VMEM((tile, N_HEADS * dv), BF16),
            pltpu.VMEM(w_in.shape, BF16),
            pltpu.VMEM(w_co.shape, BF16),
            pltpu.VMEM(w_ro.shape, BF16),
            *_stage_scratch(w_in.shape[1]),
            *_stage_scratch(d),
        ],
        compiler_params=_params(),
        name="token_branches",
    )(x2, pos, inv_freq, mod, w_in, b_in, cw, cb, clg, clb, w_co, gn_g, gn_b, w_ro,
      jnp.asarray(decay, F32), jnp.asarray(xi, F32), jnp.asarray(zeta, F32))


def _out_ffn_kernel(x_ref, m_ref, mod_ref, wout_hbm, l1g_ref, l1b_ref, wup_hbm, fw_ref, fb_ref, wdn_hbm,
                    l2g_ref, l2b_ref, o_ref, ubuf_ref, act_ref,
                    wout_ref, wup_ref, wdn_ref, stage_up, sem_up, stage_d, sem_d, *, tile, d_ff):
    n_lane_chunks = 2 * d_ff // V7X_LANES

    @pl.when(pl.program_id(0) == 0)
    def _():
        ubuf_ref[:, 0:FFN_HIST, :] = jnp.zeros((n_lane_chunks, FFN_HIST, V7X_LANES), F32)
        _load_weight_bf16(wout_hbm, wout_ref, stage_d, sem_d)
        _load_weight_bf16(wup_hbm, wup_ref, stage_up, sem_up)
        _load_weight_bf16(wdn_hbm, wdn_ref, stage_d, sem_d)

    t = jnp.dot(m_ref[...], wout_ref[...], preferred_element_type=F32)
    z1 = DEEPNORM_ALPHA * x_ref[...] + mod_ref[2:3, :] * t
    x1 = _ln(z1) * l1g_ref[...] + l1b_ref[...]

    h = _ln(x1) * (1.0 + mod_ref[4:5, :]) + mod_ref[3:4, :]
    hb = h.astype(BF16)

    def conv_cols(base):
        u = jnp.dot(hb, wup_ref[:, base:base + FFN_COLS], preferred_element_type=F32)
        outs = []
        for jj in range(FFN_COLS // V7X_LANES):
            j = base // V7X_LANES + jj
            ls = slice(j * V7X_LANES, (j + 1) * V7X_LANES)
            uj = u[:, jj * V7X_LANES:(jj + 1) * V7X_LANES]
            ubuf_ref[j, FFN_HIST:FFN_HIST + tile, :] = uj
            y = fb_ref[:, ls] + fw_ref[2:3, ls] * uj
            for k in range(FFN_CONV_KERNEL - 1):
                r0 = FFN_HIST - (FFN_CONV_KERNEL - 1) + k
                y = y + fw_ref[k:k + 1, ls] * ubuf_ref[j, r0:r0 + tile, :]
            ubuf_ref[j, 0:FFN_HIST, :] = ubuf_ref[j, tile:tile + FFN_HIST, :]
            outs.append(y)
        return jnp.concatenate(outs, axis=1)

    for gi in range(d_ff // FFN_COLS):
        val = conv_cols(gi * FFN_COLS)
        gate = conv_cols(d_ff + gi * FFN_COLS)
        act_ref[:, gi * FFN_COLS:(gi + 1) * FFN_COLS] = (val * (gate * _sigmoid(gate))).astype(BF16)

    f = jnp.dot(act_ref[...], wdn_ref[...], preferred_element_type=F32)
    z2 = DEEPNORM_ALPHA * x1 + mod_ref[5:6, :] * f
    o_ref[...] = _ln(z2) * l2g_ref[...] + l2b_ref[...]


def _out_ffn(x2, m, mod, w_out, l1g, l1b, w_up, fw, fb, w_dn, l2g, l2b, tile):
    s_len, d = x2.shape
    d_ff = w_dn.shape[0]
    kern = functools.partial(_out_ffn_kernel, tile=tile, d_ff=d_ff)
    return pl.pallas_call(
        kern,
        grid=(s_len // tile,),
        in_specs=[
            _rows(tile, d),
            _rows(tile, d),
            _resident((N_MOD, d)),
            _HBM,
            _resident((1, d)),
            _resident((1, d)),
            _HBM,
            _resident(fw.shape),
            _resident((1, 2 * d_ff)),
            _HBM,
            _resident((1, d)),
            _resident((1, d)),
        ],
        out_specs=_rows(tile, d),
        out_shape=jax.ShapeDtypeStruct((s_len, d), F32),
        scratch_shapes=[
            pltpu.VMEM((2 * d_ff // V7X_LANES, FFN_HIST + tile, V7X_LANES), F32),
            pltpu.VMEM((tile, d_ff), BF16),
            pltpu.VMEM(w_out.shape, BF16),
            pltpu.VMEM(w_up.shape, BF16),
            pltpu.VMEM(w_dn.shape, BF16),
            *_stage_scratch(w_up.shape[1]),
            *_stage_scratch(d),
        ],
        compiler_params=_params(),
        name="out_ffn",
    )(x2, m, mod, w_out, l1g, l1b, w_up, fw, fb, w_dn, l2g, l2b)


def kernel(x, c, positions, w_ada, b_ada, w_in, b_in, conv_dw_w, conv_dw_b, conv_ln_g, conv_ln_b,
           w_conv_out, ret_gn_g, ret_gn_b, w_ret_out, w_out, ln1_g, ln1_b, w_up, ffn_dw_w, ffn_dw_b,
           w_down, ln2_g, ln2_b):
    bsz, s_len, d = x.shape
    assert bsz == 1 and s_len % SEQ_TILE == 0 and d % V7X_LANES == 0
    depth = w_ada.shape[0]

    row = lambda v: v.reshape(1, -1)
    x2 = x.reshape(s_len, d)
    pos = positions.reshape(s_len, 1).astype(F32)
    for l in range(depth):
        mod = _adaln_mod(c, w_ada[l], b_ada[l])
        m = _token_branches(x2, pos, mod, w_in[l], row(b_in[l]), conv_dw_w[l], row(conv_dw_b[l]),
                            row(conv_ln_g[l]), row(conv_ln_b[l]), w_conv_out[l],
                            row(ret_gn_g[l]), row(ret_gn_b[l]), w_ret_out[l], SEQ_TILE)
        x2 = _out_ffn(x2, m, mod, w_out[l], row(ln1_g[l]), row(ln1_b[l]), w_up[l],
                      ffn_dw_w[l], row(ffn_dw_b[l]), w_down[l], row(ln2_g[l]), row(ln2_b[l]),
                      SEQ_TILE)
    return x2.reshape(bsz, s_len, d)
```

```python
import functools

import jax
import jax.numpy as jnp
import numpy as np
from jax import lax
from jax.experimental import pallas as pl
from jax.experimental.pallas import tpu as pltpu

F32 = jnp.float32
BF16 = jnp.bfloat16

DEPTH = 1
N_HEADS = 8
RET_CHUNK = 128
CONV_KERNEL = 31
FFN_CONV_KERNEL = 3
ROPE_BASE = 10000.0
LN_EPS = 1e-5
DEEPNORM_ALPHA = (2.0 * DEPTH) ** 0.25
N_MOD = 6

V7X_LANES = 128
V7X_SUBLANES = 8

SEQ_TILE = 512
CONV_HIST = 32
FFN_HIST = V7X_SUBLANES
CONV_ROWS = 128
FFN_COLS = 256
VMEM_LIMIT_BYTES = 62 * 1024 * 1024
TOKEN_STAGE_CHUNK_BYTES = 256 * 1024
FFN_STAGE_CHUNK_BYTES = 1024 * 1024
STAGE_SLOTS = 4


def _ln(x):
    mu = jnp.mean(x, axis=-1, keepdims=True)
    xc = x - mu
    var = jnp.mean(xc * xc, axis=-1, keepdims=True)
    return xc * lax.rsqrt(var + LN_EPS)


def _sigmoid(x):
    return 0.5 * jnp.tanh(0.5 * x) + 0.5


def _resident(shape):
    nd = len(shape)
    return pl.BlockSpec(shape, lambda i: (0,) * nd, pipeline_mode=pl.Buffered(1))


def _rows(tile, width):
    return pl.BlockSpec((tile, width), lambda i: (i, 0))


def _params():
    return pltpu.CompilerParams(dimension_semantics=("arbitrary",), vmem_limit_bytes=VMEM_LIMIT_BYTES)


_HBM = pl.BlockSpec(memory_space=pl.ANY)


def _stage_scratch(width, chunk_bytes):
    rows = max(chunk_bytes // (width * 4), 2 * V7X_SUBLANES)
    rows -= rows % (2 * V7X_SUBLANES)
    return [pltpu.VMEM((STAGE_SLOTS, rows, width), F32), pltpu.SemaphoreType.DMA((STAGE_SLOTS,))]


def _load_weight_bf16(w_hbm, w_vmem, stage, sem):
    n_slots, rows = stage.shape[0], stage.shape[1]
    n_chunks = w_hbm.shape[0] // rows
    assert n_chunks * rows == w_hbm.shape[0] and stage.shape[2] == w_hbm.shape[1] and n_chunks >= n_slots

    def copy(c, slot):
        r0 = pl.multiple_of(c * rows, rows)
        return pltpu.make_async_copy(w_hbm.at[pl.ds(r0, rows), :], stage.at[slot], sem.at[slot])

    for c in range(n_slots - 1):
        copy(c, c).start()

    def body(c, carry):
        slot = c % n_slots
        ahead = c + n_slots - 1

        @pl.when(ahead < n_chunks)
        def _():
            copy(ahead, ahead % n_slots).start()

        copy(c, slot).wait()
        r0 = pl.multiple_of(c * rows, rows)
        w_vmem[pl.ds(r0, rows), :] = stage[slot].astype(BF16)
        return carry

    lax.fori_loop(0, n_chunks, body, 0)


def _mod_kernel(c_ref, w_ref, b_ref, o_ref):
    c = c_ref[...]
    s = c * _sigmoid(c)
    o_ref[...] = jnp.dot(s, w_ref[...], preferred_element_type=F32) + b_ref[...]


def _adaln_mod(c, w_ada, b_ada):
    d = c.shape[-1]
    c8 = jnp.broadcast_to(c, (V7X_SUBLANES, d))
    out = pl.pallas_call(
        _mod_kernel,
        grid=(N_MOD,),
        in_specs=[
            pl.BlockSpec((V7X_SUBLANES, d), lambda i: (0, 0)),
            pl.BlockSpec((d, d), lambda i: (0, i)),
            pl.BlockSpec((1, d), lambda i: (0, i)),
        ],
        out_specs=pl.BlockSpec((V7X_SUBLANES, d), lambda i: (0, i)),
        out_shape=jax.ShapeDtypeStruct((V7X_SUBLANES, N_MOD * d), F32),
        compiler_params=pltpu.CompilerParams(dimension_semantics=("arbitrary",)),
        name="adaln_mod",
    )(c8, w_ada, b_ada.reshape(1, -1))
    return out[0].reshape(N_MOD, d)


def _branches_kernel(x_ref, pos_ref, invf_ref, mod_ref, w_hbm, b_ref, cw_ref, cb_ref, clg_ref, clb_ref,
                     wco_hbm, gng_ref, gnb_ref, wro_hbm, decay_ref, xi_ref, zeta_ref, o_ref,
                     state_ref, abuf_ref, q_ref, qx_ref, k_ref, kz_ref, v_ref, sg_ref, ract_ref,
                     w_ref, wco_ref, wro_ref, stage_in, sem_in, stage_d, sem_d,
                     *, tile, d_model, dk, dv, chunk_decay):
    n_heads = d_model // dk
    v_w = n_heads * dv
    n_lane_chunks = d_model // V7X_LANES
    n_chunks = tile // RET_CHUNK
    o_q = 0
    o_k = o_q + d_model
    o_v = o_k + d_model
    o_g = o_v + v_w
    o_cv = o_g + v_w
    o_cg = o_cv + d_model
    o_ga = o_cg + d_model
    o_gb = o_ga + d_model

    @pl.when(pl.program_id(0) == 0)
    def _():
        state_ref[...] = jnp.zeros_like(state_ref)
        abuf_ref[:, 0:CONV_HIST, :] = jnp.zeros((n_lane_chunks, CONV_HIST, V7X_LANES), F32)
        _load_weight_bf16(w_hbm, w_ref, stage_in, sem_in)
        _load_weight_bf16(wco_hbm, wco_ref, stage_d, sem_d)
        _load_weight_bf16(wro_hbm, wro_ref, stage_d, sem_d)

    x = x_ref[...]
    h = _ln(x) * (1.0 + mod_ref[1:2, :]) + mod_ref[0:1, :]
    hb = h.astype(BF16)

    def proj(lo, width):
        return jnp.dot(hb, w_ref[:, lo:lo + width], preferred_element_type=F32) + b_ref[:, lo:lo + width]

    def anchor(v):
        bits = lax.bitcast_convert_type(v[0:1, 0:V7X_LANES].astype(F32), jnp.uint32)
        bits = lax.shift_right_logical(lax.shift_right_logical(bits, jnp.uint32(16)), jnp.uint32(16))
        return lax.bitcast_convert_type(bits, F32)

    a = proj(o_cv, d_model) * _sigmoid(proj(o_cg, d_model))
    for j in range(n_lane_chunks):
        abuf_ref[j, CONV_HIST:CONV_HIST + tile, :] = a[:, j * V7X_LANES:(j + 1) * V7X_LANES]
    first = CONV_HIST - (CONV_KERNEL - 1)

    def conv_chunk(j, zero):
        ls = slice(j * V7X_LANES, (j + 1) * V7X_LANES)
        bias = cb_ref[:, ls] if zero is None else cb_ref[:, ls] + zero
        blocks = []
        for rb in range(tile // CONV_ROWS):
            acc = jnp.broadcast_to(bias, (CONV_ROWS, V7X_LANES))
            for k in range(CONV_KERNEL):
                r0 = rb * CONV_ROWS + first + k
                acc = acc + cw_ref[k:k + 1, ls] * abuf_ref[j, r0:r0 + CONV_ROWS, :]
            blocks.append(acc)
        abuf_ref[j, 0:CONV_HIST, :] = abuf_ref[j, tile:tile + CONV_HIST, :]
        return jnp.concatenate(blocks, axis=0)

    ang = pos_ref[...] * invf_ref[...]
    cs = jnp.cos(ang)
    sn = jnp.sin(ang)
    lane = lax.broadcasted_iota(jnp.int32, ang.shape, 1)
    sn = jnp.where(lane < dk // 2, -sn, sn)
    k_scale = float(dk) ** -0.5

    def rope_heads(y, cos_t, sin_t):
        for hh in range(n_heads):
            yh = y[:, hh * dk:(hh + 1) * dk]
            yield hh, yh * cos_t + pltpu.roll(yh, dk // 2, 1) * sin_t

    def tiled(tab_ref, hh):
        return jnp.concatenate([tab_ref[:, hh * dk:(hh + 1) * dk]] * n_chunks, axis=0)

    half_v = v_w // 2
    vb0 = proj(o_v, half_v).astype(BF16)
    v_ref[:, 0:half_v] = vb0
    vb1 = proj(o_v + half_v, half_v).astype(BF16)
    v_ref[:, half_v:v_w] = vb1
    pins = {3: anchor(vb0), 6: anchor(vb1)}
    cols = [conv_chunk(j, pins.get(j)) for j in range(n_lane_chunks)]
    for hh, r in rope_heads(proj(o_q, d_model), cs, sn):
        q_ref[:, hh * dk:(hh + 1) * dk] = r.astype(BF16)
        qx_ref[:, hh * dk:(hh + 1) * dk] = (r * tiled(xi_ref, hh)).astype(BF16)
    for hh, r in rope_heads(proj(o_k, d_model), cs * k_scale, sn * k_scale):
        k_ref[:, hh * dk:(hh + 1) * dk] = r.astype(BF16)
        kz_ref[:, hh * dk:(hh + 1) * dk] = (r * tiled(zeta_ref, hh)).astype(BF16)
    g = proj(o_g, half_v)
    sg_ref[:, 0:half_v] = g * _sigmoid(g)
    g = proj(o_g + half_v, half_v)
    sg_ref[:, half_v:v_w] = g * _sigmoid(g)
    ga = _sigmoid(proj(o_ga, d_model))
    gb = _sigmoid(proj(o_gb, d_model))

    an = _ln(jnp.concatenate(cols, axis=1)) * clg_ref[...] + clb_ref[...]
    sa = (an * _sigmoid(an)).astype(BF16)
    ua = ga * jnp.dot(sa, wco_ref[...], preferred_element_type=F32)

    heads = range(n_heads)
    for ci in range(n_chunks):
        rows = slice(ci * RET_CHUNK, (ci + 1) * RET_CHUNK)
        qs = [slice(hh * dk, (hh + 1) * dk) for hh in heads]
        vs = [slice(hh * dv, (hh + 1) * dv) for hh in heads]
        sc = [lax.dot_general(q_ref[rows, qs[hh]], k_ref[rows, qs[hh]], (((1,), (1,)), ((), ())),
                              preferred_element_type=F32) for hh in heads]
        lhs = [jnp.concatenate([(sc[hh] * decay_ref[hh]).astype(BF16), qx_ref[rows, qs[hh]]], axis=1)
               for hh in heads]
        st = [state_ref[hh] for hh in heads]
        vh = [v_ref[rows, vs[hh]] for hh in heads]
        o = [jnp.dot(lhs[hh], jnp.concatenate([vh[hh], st[hh].astype(BF16)], axis=0),
                     preferred_element_type=F32) for hh in heads]
        kv = [lax.dot_general(kz_ref[rows, qs[hh]], vh[hh], (((0,), (0,)), ((), ())),
                              preferred_element_type=F32) for hh in heads]
        for hh in heads:
            state_ref[hh] = st[hh] * chunk_decay[hh] + kv[hh]
        for hh in heads:
            r = (_ln(o[hh]) * gng_ref[:, vs[hh]] + gnb_ref[:, vs[hh]]) * sg_ref[rows, vs[hh]]
            ract_ref[rows, vs[hh]] = r.astype(BF16)

    yb = jnp.dot(ract_ref[...], wro_ref[...], preferred_element_type=F32)
    o_ref[...] = (ua + gb * yb).astype(BF16)


def _token_branches(x2, pos, mod, w_in, b_in, cw, cb, clg, clb, w_co, gn_g, gn_b, w_ro, tile):
    s_len, d = x2.shape
    dk = d // N_HEADS
    dv = w_ro.shape[0] // N_HEADS
    log_gamma = np.log(1.0 - 2.0 ** (-5.0 - np.arange(N_HEADS, dtype=np.float64)))
    idx = np.arange(RET_CHUNK, dtype=np.float64)
    rel = idx[:, None] - idx[None, :]
    decay = np.where(rel[None] >= 0, np.exp(log_gamma[:, None, None] * np.maximum(rel, 0.0)[None]), 0.0)
    xi = np.repeat(np.exp(log_gamma[None, :] * (idx[:, None] + 1.0)), dk, axis=1)
    zeta = np.repeat(np.exp(log_gamma[None, :] * (RET_CHUNK - 1.0 - idx[:, None])), dk, axis=1)
    chunk_decay = tuple(float(v) for v in np.exp(log_gamma * RET_CHUNK))
    half = dk // 2
    inv_freq = ROPE_BASE ** (-jnp.arange(half, dtype=F32) / half)
    inv_freq = jnp.concatenate([inv_freq, inv_freq]).reshape(1, dk)

    kern = functools.partial(_branches_kernel, tile=tile, d_model=d, dk=dk, dv=dv, chunk_decay=chunk_decay)
    return pl.pallas_call(
        kern,
        grid=(s_len // tile,),
        in_specs=[
            _rows(tile, d),
            _rows(tile, 1),
            _resident((1, dk)),
            _resident((N_MOD, d)),
            _HBM,
            _resident(b_in.shape),
            _resident(cw.shape),
            _resident((1, d)),
            _resident((1, d)),
            _resident((1, d)),
            _HBM,
            _resident((1, N_HEADS * dv)),
            _resident((1, N_HEADS * dv)),
            _HBM,
            _resident((N_HEADS, RET_CHUNK, RET_CHUNK)),
            _resident((RET_CHUNK, d)),
            _resident((RET_CHUNK, d)),
        ],
        out_specs=_rows(tile, d),
        out_shape=jax.ShapeDtypeStruct((s_len, d), BF16),
        scratch_shapes=[
            pltpu.VMEM((N_HEADS, dk, dv), F32),
            pltpu.VMEM((d // V7X_LANES, CONV_HIST + tile, V7X_LANES), F32),
            pltpu.VMEM((tile, d), BF16),
            pltpu.VMEM((tile, d), BF16),
            pltpu.VMEM((tile, d), BF16),
            pltpu.VMEM((tile, d), BF16),
            pltpu.VMEM((tile, N_HEADS * dv), BF16),
            pltpu.VMEM((tile, N_HEADS * dv), F32),
            pltpu.VMEM((tile, N_HEADS * dv), BF16),
            pltpu.VMEM(w_in.shape, BF16),
            pltpu.VMEM(w_co.shape, BF16),
            pltpu.VMEM(w_ro.shape, BF16),
            *_stage_scratch(w_in.shape[1], TOKEN_STAGE_CHUNK_BYTES),
            *_stage_scratch(d, TOKEN_STAGE_CHUNK_BYTES),
        ],
        compiler_params=_params(),
        name="token_branches",
    )(x2, pos, inv_freq, mod, w_in, b_in, cw, cb, clg, clb, w_co, gn_g, gn_b, w_ro,
      jnp.asarray(decay, F32), jnp.asarray(xi, F32), jnp.asarray(zeta, F32))


def _out_ffn_kernel(x_ref, m_ref, mod_ref, wout_hbm, l1g_ref, l1b_ref, wup_hbm, fw_ref, fb_ref, wdn_hbm,
                    l2g_ref, l2b_ref, o_ref, ubuf_ref, act_ref,
                    wout_ref, wup_ref, wdn_ref, stage_up, sem_up, stage_d, sem_d, *, tile, d_ff):
    n_lane_chunks = 2 * d_ff // V7X_LANES

    @pl.when(pl.program_id(0) == 0)
    def _():
        ubuf_ref[:, 0:FFN_HIST, :] = jnp.zeros((n_lane_chunks, FFN_HIST, V7X_LANES), F32)
        _load_weight_bf16(wout_hbm, wout_ref, stage_d, sem_d)
        _load_weight_bf16(wup_hbm, wup_ref, stage_up, sem_up)
        _load_weight_bf16(wdn_hbm, wdn_ref, stage_d, sem_d)

    t = jnp.dot(m_ref[...], wout_ref[...], preferred_element_type=F32)
    z1 = DEEPNORM_ALPHA * x_ref[...] + mod_ref[2:3, :] * t
    x1 = _ln(z1) * l1g_ref[...] + l1b_ref[...]

    h = _ln(x1) * (1.0 + mod_ref[4:5, :]) + mod_ref[3:4, :]
    hb = h.astype(BF16)

    def conv_cols(base):
        u = jnp.dot(hb, wup_ref[:, base:base + FFN_COLS], preferred_element_type=F32)
        outs = []
        for jj in range(FFN_COLS // V7X_LANES):
            j = base // V7X_LANES + jj
            ls = slice(j * V7X_LANES, (j + 1) * V7X_LANES)
            uj = u[:, jj * V7X_LANES:(jj + 1) * V7X_LANES]
            ubuf_ref[j, FFN_HIST:FFN_HIST + tile, :] = uj
            y = fb_ref[:, ls] + fw_ref[2:3, ls] * uj
            for k in range(FFN_CONV_KERNEL - 1):
                r0 = FFN_HIST - (FFN_CONV_KERNEL - 1) + k
                y = y + fw_ref[k:k + 1, ls] * ubuf_ref[j, r0:r0 + tile, :]
            ubuf_ref[j, 0:FFN_HIST, :] = ubuf_ref[j, tile:tile + FFN_HIST, :]
            outs.append(y)
        return jnp.concatenate(outs, axis=1)

    for gi in range(d_ff // FFN_COLS):
        val = conv_cols(gi * FFN_COLS)
        gate = conv_cols(d_ff + gi * FFN_COLS)
        act_ref[:, gi * FFN_COLS:(gi + 1) * FFN_COLS] = (val * (gate * _sigmoid(gate))).astype(BF16)

    f = jnp.dot(act_ref[...], wdn_ref[...], preferred_element_type=F32)
    z2 = DEEPNORM_ALPHA * x1 + mod_ref[5:6, :] * f
    o_ref[...] = _ln(z2) * l2g_ref[...] + l2b_ref[...]


def _out_ffn(x2, m, mod, w_out, l1g, l1b, w_up, fw, fb, w_dn, l2g, l2b, tile):
    s_len, d = x2.shape
    d_ff = w_dn.shape[0]
    kern = functools.partial(_out_ffn_kernel, tile=tile, d_ff=d_ff)
    return pl.pallas_call(
        kern,
        grid=(s_len // tile,),
        in_specs=[
            _rows(tile, d),
            _rows(tile, d),
            _resident((N_MOD, d)),
            _HBM,
            _resident((1, d)),
            _resident((1, d)),
            _HBM,
            _resident(fw.shape),
            _resident((1, 2 * d_ff)),
            _HBM,
            _resident((1, d)),
            _resident((1, d)),
        ],
        out_specs=_rows(tile, d),
        out_shape=jax.ShapeDtypeStruct((s_len, d), F32),
        scratch_shapes=[
            pltpu.VMEM((2 * d_ff // V7X_LANES, FFN_HIST + tile, V7X_LANES), F32),
            pltpu.VMEM((tile, d_ff), BF16),
            pltpu.VMEM(w_out.shape, BF16),
            pltpu.VMEM(w_up.shape, BF16),
            pltpu.VMEM(w_dn.shape, BF16),
            *_stage_scratch(w_up.shape[1], FFN_STAGE_CHUNK_BYTES),
            *_stage_scratch(d, FFN_STAGE_CHUNK_BYTES),
        ],
        compiler_params=_params(),
        name="out_ffn",
    )(x2, m, mod, w_out, l1g, l1b, w_up, fw, fb, w_dn, l2g, l2b)


def kernel(x, c, positions, w_ada, b_ada, w_in, b_in, conv_dw_w, conv_dw_b, conv_ln_g, conv_ln_b,
           w_conv_out, ret_gn_g, ret_gn_b, w_ret_out, w_out, ln1_g, ln1_b, w_up, ffn_dw_w, ffn_dw_b,
           w_down, ln2_g, ln2_b):
    bsz, s_len, d = x.shape
    assert bsz == 1 and s_len % SEQ_TILE == 0 and d % V7X_LANES == 0
    depth = w_ada.shape[0]

    row = lambda v: v.reshape(1, -1)
    x2 = x.reshape(s_len, d)
    pos = positions.reshape(s_len, 1).astype(F32)
    for l in range(depth):
        mod = _adaln_mod(c, w_ada[l], b_ada[l])
        m = _token_branches(x2, pos, mod, w_in[l], row(b_in[l]), conv_dw_w[l], row(conv_dw_b[l]),
                            row(conv_ln_g[l]), row(conv_ln_b[l]), w_conv_out[l],
                            row(ret_gn_g[l]), row(ret_gn_b[l]), w_ret_out[l], SEQ_TILE)
        x2 = _out_ffn(x2, m, mod, w_out[l], row(ln1_g[l]), row(ln1_b[l]), w_up[l],
                      ffn_dw_w[l], row(ffn_dw_b[l]), w_down[l], row(ln2_g[l]), row(ln2_b[l]),
                      SEQ_TILE)
    return x2.reshape(bsz, s_len, d)
```

```python
import functools

import jax
import jax.numpy as jnp
import numpy as np
from jax import lax
from jax.experimental import pallas as pl
from jax.experimental.pallas import tpu as pltpu

F32 = jnp.float32
BF16 = jnp.bfloat16

DEPTH = 1
N_HEADS = 8
RET_CHUNK = 128
CONV_KERNEL = 31
FFN_CONV_KERNEL = 3
ROPE_BASE = 10000.0
LN_EPS = 1e-5
DEEPNORM_ALPHA = (2.0 * DEPTH) ** 0.25
N_MOD = 6

V7X_LANES = 128
V7X_SUBLANES = 8

SEQ_TILE = 512
CONV_HIST = 32
FFN_HIST = V7X_SUBLANES
CONV_ROWS = 128
FFN_COLS = 256
VMEM_LIMIT_BYTES = 63 * 1024 * 1024
TOKEN_STAGE_CHUNK_BYTES = 256 * 1024
FFN_STAGE_CHUNK_BYTES = 1024 * 1024
STAGE_SLOTS = 4


def _ln(x):
    mu = jnp.mean(x, axis=-1, keepdims=True)
    xc = x - mu
    var = jnp.mean(xc * xc, axis=-1, keepdims=True)
    return xc * lax.rsqrt(var + LN_EPS)


def _sigmoid(x):
    return 0.5 * jnp.tanh(0.5 * x) + 0.5


def _resident(shape):
    nd = len(shape)
    return pl.BlockSpec(shape, lambda i: (0,) * nd, pipeline_mode=pl.Buffered(1))


def _rows(tile, width):
    return pl.BlockSpec((tile, width), lambda i: (i, 0))


def _params():
    return pltpu.CompilerParams(dimension_semantics=("arbitrary",), vmem_limit_bytes=VMEM_LIMIT_BYTES)


_HBM = pl.BlockSpec(memory_space=pl.ANY)


def _stage_scratch(width, chunk_bytes, slots=STAGE_SLOTS):
    rows = max(chunk_bytes // (width * 4), 2 * V7X_SUBLANES)
    rows -= rows % (2 * V7X_SUBLANES)
    return [pltpu.VMEM((slots, rows, width), F32), pltpu.SemaphoreType.DMA((slots,))]


def _load_weight_bf16(w_hbm, w_vmem, stage, sem):
    n_slots, rows = stage.shape[0], stage.shape[1]
    n_chunks = w_hbm.shape[0] // rows
    assert n_chunks * rows == w_hbm.shape[0] and stage.shape[2] == w_hbm.shape[1] and n_chunks >= n_slots

    def copy(c, slot):
        r0 = pl.multiple_of(c * rows, rows)
        return pltpu.make_async_copy(w_hbm.at[pl.ds(r0, rows), :], stage.at[slot], sem.at[slot])

    for c in range(n_slots - 1):
        copy(c, c).start()

    def body(c, carry):
        slot = c % n_slots
        ahead = c + n_slots - 1

        @pl.when(ahead < n_chunks)
        def _():
            copy(ahead, ahead % n_slots).start()

        copy(c, slot).wait()
        r0 = pl.multiple_of(c * rows, rows)
        w_vmem[pl.ds(r0, rows), :] = stage[slot].astype(BF16)
        return carry

    lax.fori_loop(0, n_chunks, body, 0)


def _mod_kernel(c_ref, w_ref, b_ref, o_ref):
    c = c_ref[...]
    s = c * _sigmoid(c)
    o_ref[...] = jnp.dot(s, w_ref[...], preferred_element_type=F32) + b_ref[...]


def _adaln_mod(c, w_ada, b_ada):
    d = c.shape[-1]
    c8 = jnp.broadcast_to(c, (V7X_SUBLANES, d))
    out = pl.pallas_call(
        _mod_kernel,
        grid=(N_MOD,),
        in_specs=[
            pl.BlockSpec((V7X_SUBLANES, d), lambda i: (0, 0)),
            pl.BlockSpec((d, d), lambda i: (0, i)),
            pl.BlockSpec((1, d), lambda i: (0, i)),
        ],
        out_specs=pl.BlockSpec((V7X_SUBLANES, d), lambda i: (0, i)),
        out_shape=jax.ShapeDtypeStruct((V7X_SUBLANES, N_MOD * d), F32),
        compiler_params=pltpu.CompilerParams(dimension_semantics=("arbitrary",)),
        name="adaln_mod",
    )(c8, w_ada, b_ada.reshape(1, -1))
    return out[0].reshape(N_MOD, d)


def _branches_kernel(x_ref, pos_ref, invf_ref, mod_ref, w_hbm, b_ref, cw_ref, cb_ref, clg_ref, clb_ref,
                     wco_hbm, gng_ref, gnb_ref, wro_hbm, decay_ref, xi_ref, zeta_ref, o_ref,
                     state_ref, abuf_ref, q_ref, qx_ref, k_ref, kz_ref, v_ref, sg_ref, ract_ref,
                     w_ref, wco_ref, wro_ref, stage_in, sem_in, stage_d, sem_d,
                     *, tile, d_model, dk, dv, chunk_decay):
    n_heads = d_model // dk
    v_w = n_heads * dv
    n_lane_chunks = d_model // V7X_LANES
    n_chunks = tile // RET_CHUNK
    o_q = 0
    o_k = o_q + d_model
    o_v = o_k + d_model
    o_g = o_v + v_w
    o_cv = o_g + v_w
    o_cg = o_cv + d_model
    o_ga = o_cg + d_model
    o_gb = o_ga + d_model

    @pl.when(pl.program_id(0) == 0)
    def _():
        state_ref[...] = jnp.zeros_like(state_ref)
        abuf_ref[:, 0:CONV_HIST, :] = jnp.zeros((n_lane_chunks, CONV_HIST, V7X_LANES), F32)
        _load_weight_bf16(w_hbm, w_ref, stage_in, sem_in)
        _load_weight_bf16(wco_hbm, wco_ref, stage_d, sem_d)
        _load_weight_bf16(wro_hbm, wro_ref, stage_d, sem_d)

    x = x_ref[...]
    h = _ln(x) * (1.0 + mod_ref[1:2, :]) + mod_ref[0:1, :]
    hb = h.astype(BF16)

    def proj(lo, width):
        return jnp.dot(hb, w_ref[:, lo:lo + width], preferred_element_type=F32) + b_ref[:, lo:lo + width]

    def anchor(v):
        bits = lax.bitcast_convert_type(v[0:1, 0:V7X_LANES].astype(F32), jnp.uint32)
        bits = lax.shift_right_logical(lax.shift_right_logical(bits, jnp.uint32(16)), jnp.uint32(16))
        return lax.bitcast_convert_type(bits, F32)

    a = proj(o_cv, d_model) * _sigmoid(proj(o_cg, d_model))
    for j in range(n_lane_chunks):
        abuf_ref[j, CONV_HIST:CONV_HIST + tile, :] = a[:, j * V7X_LANES:(j + 1) * V7X_LANES]
    first = CONV_HIST - (CONV_KERNEL - 1)

    def conv_chunk(j, zero):
        ls = slice(j * V7X_LANES, (j + 1) * V7X_LANES)
        bias = cb_ref[:, ls] if zero is None else cb_ref[:, ls] + zero
        blocks = []
        for rb in range(tile // CONV_ROWS):
            acc = jnp.broadcast_to(bias, (CONV_ROWS, V7X_LANES))
            for k in range(CONV_KERNEL):
                r0 = rb * CONV_ROWS + first + k
                acc = acc + cw_ref[k:k + 1, ls] * abuf_ref[j, r0:r0 + CONV_ROWS, :]
            blocks.append(acc)
        abuf_ref[j, 0:CONV_HIST, :] = abuf_ref[j, tile:tile + CONV_HIST, :]
        return jnp.concatenate(blocks, axis=0)

    ang = pos_ref[...] * invf_ref[...]
    cs = jnp.cos(ang)
    sn = jnp.sin(ang)
    lane = lax.broadcasted_iota(jnp.int32, ang.shape, 1)
    sn = jnp.where(lane < dk // 2, -sn, sn)
    k_scale = float(dk) ** -0.5

    def rope_heads(y, cos_t, sin_t):
        for hh in range(n_heads):
            yh = y[:, hh * dk:(hh + 1) * dk]
            yield hh, yh * cos_t + pltpu.roll(yh, dk // 2, 1) * sin_t

    def tiled(tab_ref, hh):
        return jnp.concatenate([tab_ref[:, hh * dk:(hh + 1) * dk]] * n_chunks, axis=0)

    half_v = v_w // 2
    vb0 = proj(o_v, half_v).astype(BF16)
    v_ref[:, 0:half_v] = vb0
    vb1 = proj(o_v + half_v, half_v).astype(BF16)
    v_ref[:, half_v:v_w] = vb1
    pins = {3: anchor(vb0), 6: anchor(vb1)}
    cols = [conv_chunk(j, pins.get(j)) for j in range(n_lane_chunks)]
    for hh, r in rope_heads(proj(o_q, d_model), cs, sn):
        q_ref[:, hh * dk:(hh + 1) * dk] = r.astype(BF16)
        qx_ref[:, hh * dk:(hh + 1) * dk] = (r * tiled(xi_ref, hh)).astype(BF16)
    for hh, r in rope_heads(proj(o_k, d_model), cs * k_scale, sn * k_scale):
        k_ref[:, hh * dk:(hh + 1) * dk] = r.astype(BF16)
        kz_ref[:, hh * dk:(hh + 1) * dk] = (r * tiled(zeta_ref, hh)).astype(BF16)
    g = proj(o_g, half_v)
    sg_ref[:, 0:half_v] = g * _sigmoid(g)
    g = proj(o_g + half_v, half_v)
    sg_ref[:, half_v:v_w] = g * _sigmoid(g)
    ga = _sigmoid(proj(o_ga, d_model))
    gb = _sigmoid(proj(o_gb, d_model))

    an = _ln(jnp.concatenate(cols, axis=1)) * clg_ref[...] + clb_ref[...]
    sa = (an * _sigmoid(an)).astype(BF16)
    ua = ga * jnp.dot(sa, wco_ref[...], preferred_element_type=F32)

    heads = range(n_heads)
    for ci in range(n_chunks):
        rows = slice(ci * RET_CHUNK, (ci + 1) * RET_CHUNK)
        qs = [slice(hh * dk, (hh + 1) * dk) for hh in heads]
        vs = [slice(hh * dv, (hh + 1) * dv) for hh in heads]
        sc = [lax.dot_general(q_ref[rows, qs[hh]], k_ref[rows, qs[hh]], (((1,), (1,)), ((), ())),
                              preferred_element_type=F32) for hh in heads]
        lhs = [jnp.concatenate([(sc[hh] * decay_ref[hh]).astype(BF16), qx_ref[rows, qs[hh]]], axis=1)
               for hh in heads]
        st = [state_ref[hh] for hh in heads]
        vh = [v_ref[rows, vs[hh]] for hh in heads]
        o = [jnp.dot(lhs[hh], jnp.concatenate([vh[hh], st[hh].astype(BF16)], axis=0),
                     preferred_element_type=F32) for hh in heads]
        kv = [lax.dot_general(kz_ref[rows, qs[hh]], vh[hh], (((0,), (0,)), ((), ())),
                              preferred_element_type=F32) for hh in heads]
        for hh in heads:
            state_ref[hh] = st[hh] * chunk_decay[hh] + kv[hh]
        for hh in heads:
            r = (_ln(o[hh]) * gng_ref[:, vs[hh]] + gnb_ref[:, vs[hh]]) * sg_ref[rows, vs[hh]]
            ract_ref[rows, vs[hh]] = r.astype(BF16)

    yb = jnp.dot(ract_ref[...], wro_ref[...], preferred_element_type=F32)
    o_ref[...] = (ua + gb * yb).astype(BF16)


def _token_branches(x2, pos, mod, w_in, b_in, cw, cb, clg, clb, w_co, gn_g, gn_b, w_ro, tile):
    s_len, d = x2.shape
    dk = d // N_HEADS
    dv = w_ro.shape[0] // N_HEADS
    log_gamma = np.log(1.0 - 2.0 ** (-5.0 - np.arange(N_HEADS, dtype=np.float64)))
    idx = np.arange(RET_CHUNK, dtype=np.float64)
    rel = idx[:, None] - idx[None, :]
    decay = np.where(rel[None] >= 0, np.exp(log_gamma[:, None, None] * np.maximum(rel, 0.0)[None]), 0.0)
    xi = np.repeat(np.exp(log_gamma[None, :] * (idx[:, None] + 1.0)), dk, axis=1)
    zeta = np.repeat(np.exp(log_gamma[None, :] * (RET_CHUNK - 1.0 - idx[:, None])), dk, axis=1)
    chunk_decay = tuple(float(v) for v in np.exp(log_gamma * RET_CHUNK))
    half = dk // 2
    inv_freq = ROPE_BASE ** (-jnp.arange(half, dtype=F32) / half)
    inv_freq = jnp.concatenate([inv_freq, inv_freq]).reshape(1, dk)

    kern = functools.partial(_branches_kernel, tile=tile, d_model=d, dk=dk, dv=dv, chunk_decay=chunk_decay)
    return pl.pallas_call(
        kern,
        grid=(s_len // tile,),
        in_specs=[
            _rows(tile, d),
            _rows(tile, 1),
            _resident((1, dk)),
            _resident((N_MOD, d)),
            _HBM,
            _resident(b_in.shape),
            _resident(cw.shape),
            _resident((1, d)),
            _resident((1, d)),
            _resident((1, d)),
            _HBM,
            _resident((1, N_HEADS * dv)),
            _resident((1, N_HEADS * dv)),
            _HBM,
            _resident((N_HEADS, RET_CHUNK, RET_CHUNK)),
            _resident((RET_CHUNK, d)),
            _resident((RET_CHUNK, d)),
        ],
        out_specs=_rows(tile, d),
        out_shape=jax.ShapeDtypeStruct((s_len, d), BF16),
        scratch_shapes=[
            pltpu.VMEM((N_HEADS, dk, dv), F32),
            pltpu.VMEM((d // V7X_LANES, CONV_HIST + tile, V7X_LANES), F32),
            pltpu.VMEM((tile, d), BF16),
            pltpu.VMEM((tile, d), BF16),
            pltpu.VMEM((tile, d), BF16),
            pltpu.VMEM((tile, d), BF16),
            pltpu.VMEM((tile, N_HEADS * dv), BF16),
            pltpu.VMEM((tile, N_HEADS * dv), F32),
            pltpu.VMEM((tile, N_HEADS * dv), BF16),
            pltpu.VMEM(w_in.shape, BF16),
            pltpu.VMEM(w_co.shape, BF16),
            pltpu.VMEM(w_ro.shape, BF16),
            *_stage_scratch(w_in.shape[1], TOKEN_STAGE_CHUNK_BYTES, slots=6),
            *_stage_scratch(d, TOKEN_STAGE_CHUNK_BYTES),
        ],
        compiler_params=_params(),
        name="token_branches",
    )(x2, pos, inv_freq, mod, w_in, b_in, cw, cb, clg, clb, w_co, gn_g, gn_b, w_ro,
      jnp.asarray(decay, F32), jnp.asarray(xi, F32), jnp.asarray(zeta, F32))


def _out_ffn_kernel(x_ref, m_ref, mod_ref, wout_hbm, l1g_ref, l1b_ref, wup_hbm, fw_ref, fb_ref, wdn_hbm,
                    l2g_ref, l2b_ref, o_ref, ubuf_ref, act_ref,
                    wout_ref, wup_ref, wdn_ref, stage_up, sem_up, stage_d, sem_d, *, tile, d_ff):
    n_lane_chunks = 2 * d_ff // V7X_LANES

    @pl.when(pl.program_id(0) == 0)
    def _():
        ubuf_ref[:, 0:FFN_HIST, :] = jnp.zeros((n_lane_chunks, FFN_HIST, V7X_LANES), F32)
        _load_weight_bf16(wout_hbm, wout_ref, stage_d, sem_d)
        _load_weight_bf16(wup_hbm, wup_ref, stage_up, sem_up)
        _load_weight_bf16(wdn_hbm, wdn_ref, stage_d, sem_d)

    t = jnp.dot(m_ref[...], wout_ref[...], preferred_element_type=F32)
    z1 = DEEPNORM_ALPHA * x_ref[...] + mod_ref[2:3, :] * t
    x1 = _ln(z1) * l1g_ref[...] + l1b_ref[...]

    h = _ln(x1) * (1.0 + mod_ref[4:5, :]) + mod_ref[3:4, :]
    hb = h.astype(BF16)

    def conv_cols(base):
        u = jnp.dot(hb, wup_ref[:, base:base + FFN_COLS], preferred_element_type=F32)
        outs = []
        for jj in range(FFN_COLS // V7X_LANES):
            j = base // V7X_LANES + jj
            ls = slice(j * V7X_LANES, (j + 1) * V7X_LANES)
            uj = u[:, jj * V7X_LANES:(jj + 1) * V7X_LANES]
            ubuf_ref[j, FFN_HIST:FFN_HIST + tile, :] = uj
            y = fb_ref[:, ls] + fw_ref[2:3, ls] * uj
            for k in range(FFN_CONV_KERNEL - 1):
                r0 = FFN_HIST - (FFN_CONV_KERNEL - 1) + k
                y = y + fw_ref[k:k + 1, ls] * ubuf_ref[j, r0:r0 + tile, :]
            ubuf_ref[j, 0:FFN_HIST, :] = ubuf_ref[j, tile:tile + FFN_HIST, :]
            outs.append(y)
        return jnp.concatenate(outs, axis=1)

    for gi in range(d_ff // FFN_COLS):
        val = conv_cols(gi * FFN_COLS)
        gate = conv_cols(d_ff + gi * FFN_COLS)
        act_ref[:, gi * FFN_COLS:(gi + 1) * FFN_COLS] = (val * (gate * _sigmoid(gate))).astype(BF16)

    f = jnp.dot(act_ref[...], wdn_ref[...], preferred_element_type=F32)
    z2 = DEEPNORM_ALPHA * x1 + mod_ref[5:6, :] * f
    o_ref[...] = _ln(z2) * l2g_ref[...] + l2b_ref[...]


def _out_ffn(x2, m, mod, w_out, l1g, l1b, w_up, fw, fb, w_dn, l2g, l2b, tile):
    s_len, d = x2.shape
    d_ff = w_dn.shape[0]
    kern = functools.partial(_out_ffn_kernel, tile=tile, d_ff=d_ff)
    return pl.pallas_call(
        kern,
        grid=(s_len // tile,),
        in_specs=[
            _rows(tile, d),
            _rows(tile, d),
            _resident((N_MOD, d)),
            _HBM,
            _resident((1, d)),
            _resident((1, d)),
            _HBM,
            _resident(fw.shape),
            _resident((1, 2 * d_ff)),
            _HBM,
            _resident((1, d)),
            _resident((1, d)),
        ],
        out_specs=_rows(tile, d),
        out_shape=jax.ShapeDtypeStruct((s_len, d), F32),
        scratch_shapes=[
            pltpu.VMEM((2 * d_ff // V7X_LANES, FFN_HIST + tile, V7X_LANES), F32),
            pltpu.VMEM((tile, d_ff), BF16),
            pltpu.VMEM(w_out.shape, BF16),
            pltpu.VMEM(w_up.shape, BF16),
            pltpu.VMEM(w_dn.shape, BF16),
            *_stage_scratch(w_up.shape[1], FFN_STAGE_CHUNK_BYTES),
            *_stage_scratch(d, FFN_STAGE_CHUNK_BYTES),
        ],
        compiler_params=_params(),
        name="out_ffn",
    )(x2, m, mod, w_out, l1g, l1b, w_up, fw, fb, w_dn, l2g, l2b)


def kernel(x, c, positions, w_ada, b_ada, w_in, b_in, conv_dw_w, conv_dw_b, conv_ln_g, conv_ln_b,
           w_conv_out, ret_gn_g, ret_gn_b, w_ret_out, w_out, ln1_g, ln1_b, w_up, ffn_dw_w, ffn_dw_b,
           w_down, ln2_g, ln2_b):
    bsz, s_len, d = x.shape
    assert bsz == 1 and s_len % SEQ_TILE == 0 and d % V7X_LANES == 0
    depth = w_ada.shape[0]

    row = lambda v: v.reshape(1, -1)
    x2 = x.reshape(s_len, d)
    pos = positions.reshape(s_len, 1).astype(F32)
    for l in range(depth):
        mod = _adaln_mod(c, w_ada[l], b_ada[l])
        m = _token_branches(x2, pos, mod, w_in[l], row(b_in[l]), conv_dw_w[l], row(conv_dw_b[l]),
                            row(conv_ln_g[l]), row(conv_ln_b[l]), w_conv_out[l],
                            row(ret_gn_g[l]), row(ret_gn_b[l]), w_ret_out[l], SEQ_TILE)
        x2 = _out_ffn(x2, m, mod, w_out[l], row(ln1_g[l]), row(ln1_b[l]), w_up[l],
                      ffn_dw_w[l], row(ffn_dw_b[l]), w_down[l], row(ln2_g[l]), row(ln2_b[l]),
                      SEQ_TILE)
    return x2.reshape(bsz, s_len, d)
```

```python
import functools

import jax
import jax.numpy as jnp
import numpy as np
from jax import lax
from jax.experimental import pallas as pl
from jax.experimental.pallas import tpu as pltpu

F32 = jnp.float32
BF16 = jnp.bfloat16

DEPTH = 1
N_HEADS = 8
RET_CHUNK = 128
CONV_KERNEL = 31
FFN_CONV_KERNEL = 3
ROPE_BASE = 10000.0
LN_EPS = 1e-5
DEEPNORM_ALPHA = (2.0 * DEPTH) ** 0.25
N_MOD = 6

V7X_LANES = 128
V7X_SUBLANES = 8

SEQ_TILE = 512
CONV_HIST = 32
FFN_HIST = V7X_SUBLANES
CONV_ROWS = 128
FFN_COLS = 256
VMEM_LIMIT_BYTES = 63 * 1024 * 1024
TOKEN_STAGE_CHUNK_BYTES = 256 * 1024
FFN_STAGE_CHUNK_BYTES = 1024 * 1024
STAGE_SLOTS = 4


def _ln(x):
    mu = jnp.mean(x, axis=-1, keepdims=True)
    xc = x - mu
    var = jnp.mean(xc * xc, axis=-1, keepdims=True)
    return xc * lax.rsqrt(var + LN_EPS)


def _sigmoid(x):
    return 0.5 * jnp.tanh(0.5 * x) + 0.5


def _resident(shape):
    nd = len(shape)
    return pl.BlockSpec(shape, lambda i: (0,) * nd, pipeline_mode=pl.Buffered(1))


def _rows(tile, width):
    return pl.BlockSpec((tile, width), lambda i: (i, 0))


def _params():
    return pltpu.CompilerParams(dimension_semantics=("arbitrary",), vmem_limit_bytes=VMEM_LIMIT_BYTES)


_HBM = pl.BlockSpec(memory_space=pl.ANY)


def _stage_scratch(width, chunk_bytes, slots=STAGE_SLOTS):
    rows = max(chunk_bytes // (width * 4), 2 * V7X_SUBLANES)
    rows -= rows % (2 * V7X_SUBLANES)
    return [pltpu.VMEM((slots, rows, width), F32), pltpu.SemaphoreType.DMA((slots,))]


def _load_weight_bf16(w_hbm, w_vmem, stage, sem):
    n_slots, rows = stage.shape[0], stage.shape[1]
    n_chunks = w_hbm.shape[0] // rows
    assert n_chunks * rows == w_hbm.shape[0] and stage.shape[2] == w_hbm.shape[1] and n_chunks >= n_slots

    def copy(c, slot):
        r0 = pl.multiple_of(c * rows, rows)
        return pltpu.make_async_copy(w_hbm.at[pl.ds(r0, rows), :], stage.at[slot], sem.at[slot])

    for c in range(n_slots - 1):
        copy(c, c).start()

    def body(c, carry):
        slot = c % n_slots
        ahead = c + n_slots - 1

        @pl.when(ahead < n_chunks)
        def _():
            copy(ahead, ahead % n_slots).start()

        copy(c, slot).wait()
        r0 = pl.multiple_of(c * rows, rows)
        w_vmem[pl.ds(r0, rows), :] = stage[slot].astype(BF16)
        return carry

    lax.fori_loop(0, n_chunks, body, 0)


def _mod_kernel(c_ref, w_ref, b_ref, o_ref):
    c = c_ref[...]
    s = c * _sigmoid(c)
    o_ref[...] = jnp.dot(s, w_ref[...], preferred_element_type=F32) + b_ref[...]


def _adaln_mod(c, w_ada, b_ada):
    d = c.shape[-1]
    c8 = jnp.broadcast_to(c, (V7X_SUBLANES, d))
    out = pl.pallas_call(
        _mod_kernel,
        grid=(N_MOD,),
        in_specs=[
            pl.BlockSpec((V7X_SUBLANES, d), lambda i: (0, 0)),
            pl.BlockSpec((d, d), lambda i: (0, i)),
            pl.BlockSpec((1, d), lambda i: (0, i)),
        ],
        out_specs=pl.BlockSpec((V7X_SUBLANES, d), lambda i: (0, i)),
        out_shape=jax.ShapeDtypeStruct((V7X_SUBLANES, N_MOD * d), F32),
        compiler_params=pltpu.CompilerParams(dimension_semantics=("arbitrary",)),
        name="adaln_mod",
    )(c8, w_ada, b_ada.reshape(1, -1))
    return out[0].reshape(N_MOD, d)


def _branches_kernel(x_ref, pos_ref, invf_ref, mod_ref, w_hbm, b_ref, cw_ref, cb_ref, clg_ref, clb_ref,
                     wco_hbm, gng_ref, gnb_ref, wro_hbm, decay_ref, xi_ref, zeta_ref, o_ref,
                     state_ref, abuf_ref, q_ref, qx_ref, k_ref, kz_ref, v_ref, sg_ref, ract_ref,
                     w_ref, wco_ref, wro_ref, stage_in, sem_in, stage_d, sem_d,
                     *, tile, d_model, dk, dv, chunk_decay):
    n_heads = d_model // dk
    v_w = n_heads * dv
    n_lane_chunks = d_model // V7X_LANES
    n_chunks = tile // RET_CHUNK
    o_q = 0
    o_k = o_q + d_model
    o_v = o_k + d_model
    o_g = o_v + v_w
    o_cv = o_g + v_w
    o_cg = o_cv + d_model
    o_ga = o_cg + d_model
    o_gb = o_ga + d_model

    @pl.when(pl.program_id(0) == 0)
    def _():
        state_ref[...] = jnp.zeros_like(state_ref)
        abuf_ref[:, 0:CONV_HIST, :] = jnp.zeros((n_lane_chunks, CONV_HIST, V7X_LANES), F32)
        _load_weight_bf16(w_hbm, w_ref, stage_in, sem_in)
        _load_weight_bf16(wco_hbm, wco_ref, stage_d, sem_d)
        _load_weight_bf16(wro_hbm, wro_ref, stage_d, sem_d)

    x = x_ref[...]
    h = _ln(x) * (1.0 + mod_ref[1:2, :]) + mod_ref[0:1, :]
    hb = h.astype(BF16)

    def proj(lo, width):
        return jnp.dot(hb, w_ref[:, lo:lo + width], preferred_element_type=F32) + b_ref[:, lo:lo + width]

    def anchor(v):
        bits = lax.bitcast_convert_type(v[0:1, 0:V7X_LANES].astype(F32), jnp.uint32)
        bits = lax.shift_right_logical(lax.shift_right_logical(bits, jnp.uint32(16)), jnp.uint32(16))
        return lax.bitcast_convert_type(bits, F32)

    a = proj(o_cv, d_model) * _sigmoid(proj(o_cg, d_model))
    for j in range(n_lane_chunks):
        abuf_ref[j, CONV_HIST:CONV_HIST + tile, :] = a[:, j * V7X_LANES:(j + 1) * V7X_LANES]
    first = CONV_HIST - (CONV_KERNEL - 1)

    def conv_chunk(j, zero):
        ls = slice(j * V7X_LANES, (j + 1) * V7X_LANES)
        bias = cb_ref[:, ls] if zero is None else cb_ref[:, ls] + zero
        blocks = []
        for rb in range(tile // CONV_ROWS):
            acc = jnp.broadcast_to(bias, (CONV_ROWS, V7X_LANES))
            for k in range(CONV_KERNEL):
                r0 = rb * CONV_ROWS + first + k
                acc = acc + cw_ref[k:k + 1, ls] * abuf_ref[j, r0:r0 + CONV_ROWS, :]
            blocks.append(acc)
        abuf_ref[j, 0:CONV_HIST, :] = abuf_ref[j, tile:tile + CONV_HIST, :]
        return jnp.concatenate(blocks, axis=0)

    half_t = tile // 2
    lane = lax.broadcasted_iota(jnp.int32, (half_t, dk), 1)
    low = lane < dk // 2
    ang = jnp.where(low, pos_ref[0:half_t, :], pos_ref[half_t:tile, :]) * invf_ref[...]
    cs_p, sn_p = jnp.cos(ang), jnp.sin(ang)
    cs_r, sn_r = pltpu.roll(cs_p, dk // 2, 1), pltpu.roll(sn_p, dk // 2, 1)
    cs = jnp.concatenate([jnp.where(low, cs_p, cs_r), jnp.where(low, cs_r, cs_p)], axis=0)
    sn = jnp.concatenate([jnp.where(low, -sn_p, sn_r), jnp.where(low, -sn_r, sn_p)], axis=0)
    k_scale = float(dk) ** -0.5

    def rope_heads(y, cos_t, sin_t):
        for hh in range(n_heads):
            yh = y[:, hh * dk:(hh + 1) * dk]
            yield hh, yh * cos_t + pltpu.roll(yh, dk // 2, 1) * sin_t

    def tiled(tab_ref, hh):
        return jnp.concatenate([tab_ref[:, hh * dk:(hh + 1) * dk]] * n_chunks, axis=0)

    half_v = v_w // 2
    vb0 = proj(o_v, half_v).astype(BF16)
    v_ref[:, 0:half_v] = vb0
    vb1 = proj(o_v + half_v, half_v).astype(BF16)
    v_ref[:, half_v:v_w] = vb1
    pins = {3: anchor(vb0), 6: anchor(vb1)}
    cols = [conv_chunk(j, pins.get(j)) for j in range(n_lane_chunks)]
    for hh, r in rope_heads(proj(o_q, d_model), cs, sn):
        q_ref[:, hh * dk:(hh + 1) * dk] = r.astype(BF16)
        qx_ref[:, hh * dk:(hh + 1) * dk] = (r * tiled(xi_ref, hh)).astype(BF16)
    for hh, r in rope_heads(proj(o_k, d_model), cs * k_scale, sn * k_scale):
        k_ref[:, hh * dk:(hh + 1) * dk] = r.astype(BF16)
        kz_ref[:, hh * dk:(hh + 1) * dk] = (r * tiled(zeta_ref, hh)).astype(BF16)
    g = proj(o_g, half_v)
    sg_ref[:, 0:half_v] = g * _sigmoid(g)
    g = proj(o_g + half_v, half_v)
    sg_ref[:, half_v:v_w] = g * _sigmoid(g)
    ga = _sigmoid(proj(o_ga, d_model))
    gb = _sigmoid(proj(o_gb, d_model))

    an = _ln(jnp.concatenate(cols, axis=1)) * clg_ref[...] + clb_ref[...]
    sa = (an * _sigmoid(an)).astype(BF16)
    ua = ga * jnp.dot(sa, wco_ref[...], preferred_element_type=F32)

    heads = range(n_heads)
    for ci in range(n_chunks):
        rows = slice(ci * RET_CHUNK, (ci + 1) * RET_CHUNK)
        qs = [slice(hh * dk, (hh + 1) * dk) for hh in heads]
        vs = [slice(hh * dv, (hh + 1) * dv) for hh in heads]
        sc = [lax.dot_general(q_ref[rows, qs[hh]], k_ref[rows, qs[hh]], (((1,), (1,)), ((), ())),
                              preferred_element_type=F32) for hh in heads]
        lhs = [jnp.concatenate([(sc[hh] * decay_ref[hh]).astype(BF16), qx_ref[rows, qs[hh]]], axis=1)
               for hh in heads]
        st = [state_ref[hh] for hh in heads]
        vh = [v_ref[rows, vs[hh]] for hh in heads]
        o = [jnp.dot(lhs[hh], jnp.concatenate([vh[hh], st[hh].astype(BF16)], axis=0),
                     preferred_element_type=F32) for hh in heads]
        kv = [lax.dot_general(kz_ref[rows, qs[hh]], vh[hh], (((0,), (0,)), ((), ())),
                              preferred_element_type=F32) for hh in heads]
        for hh in heads:
            state_ref[hh] = st[hh] * chunk_decay[hh] + kv[hh]
        for hh in heads:
            r = (_ln(o[hh]) * gng_ref[:, vs[hh]] + gnb_ref[:, vs[hh]]) * sg_ref[rows, vs[hh]]
            ract_ref[rows, vs[hh]] = r.astype(BF16)

    yb = jnp.dot(ract_ref[...], wro_ref[...], preferred_element_type=F32)
    o_ref[...] = (ua + gb * yb).astype(BF16)


def _token_branches(x2, pos, mod, w_in, b_in, cw, cb, clg, clb, w_co, gn_g, gn_b, w_ro, tile):
    s_len, d = x2.shape
    dk = d // N_HEADS
    dv = w_ro.shape[0] // N_HEADS
    log_gamma = np.log(1.0 - 2.0 ** (-5.0 - np.arange(N_HEADS, dtype=np.float64)))
    idx = np.arange(RET_CHUNK, dtype=np.float64)
    rel = idx[:, None] - idx[None, :]
    decay = np.where(rel[None] >= 0, np.exp(log_gamma[:, None, None] * np.maximum(rel, 0.0)[None]), 0.0)
    xi = np.repeat(np.exp(log_gamma[None, :] * (idx[:, None] + 1.0)), dk, axis=1)
    zeta = np.repeat(np.exp(log_gamma[None, :] * (RET_CHUNK - 1.0 - idx[:, None])), dk, axis=1)
    chunk_decay = tuple(float(v) for v in np.exp(log_gamma * RET_CHUNK))
    half = dk // 2
    inv_freq = ROPE_BASE ** (-jnp.arange(half, dtype=F32) / half)
    inv_freq = jnp.concatenate([inv_freq, inv_freq]).reshape(1, dk)

    kern = functools.partial(_branches_kernel, tile=tile, d_model=d, dk=dk, dv=dv, chunk_decay=chunk_decay)
    return pl.pallas_call(
        kern,
        grid=(s_len // tile,),
        in_specs=[
            _rows(tile, d),
            _rows(tile, 1),
            _resident((1, dk)),
            _resident((N_MOD, d)),
            _HBM,
            _resident(b_in.shape),
            _resident(cw.shape),
            _resident((1, d)),
            _resident((1, d)),
            _resident((1, d)),
            _HBM,
            _resident((1, N_HEADS * dv)),
            _resident((1, N_HEADS * dv)),
            _HBM,
            _resident((N_HEADS, RET_CHUNK, RET_CHUNK)),
            _resident((RET_CHUNK, d)),
            _resident((RET_CHUNK, d)),
        ],
        out_specs=_rows(tile, d),
        out_shape=jax.ShapeDtypeStruct((s_len, d), BF16),
        scratch_shapes=[
            pltpu.VMEM((N_HEADS, dk, dv), F32),
            pltpu.VMEM((d // V7X_LANES, CONV_HIST + tile, V7X_LANES), F32),
            pltpu.VMEM((tile, d), BF16),
            pltpu.VMEM((tile, d), BF16),
            pltpu.VMEM((tile, d), BF16),
            pltpu.VMEM((tile, d), BF16),
            pltpu.VMEM((tile, N_HEADS * dv), BF16),
            pltpu.VMEM((tile, N_HEADS * dv), F32),
            pltpu.VMEM((tile, N_HEADS * dv), BF16),
            pltpu.VMEM(w_in.shape, BF16),
            pltpu.VMEM(w_co.shape, BF16),
            pltpu.VMEM(w_ro.shape, BF16),
            *_stage_scratch(w_in.shape[1], TOKEN_STAGE_CHUNK_BYTES, slots=6),
            *_stage_scratch(d, TOKEN_STAGE_CHUNK_BYTES),
        ],
        compiler_params=_params(),
        name="token_branches",
    )(x2, pos, inv_freq, mod, w_in, b_in, cw, cb, clg, clb, w_co, gn_g, gn_b, w_ro,
      jnp.asarray(decay, F32), jnp.asarray(xi, F32), jnp.asarray(zeta, F32))


def _out_ffn_kernel(x_ref, m_ref, mod_ref, wout_hbm, l1g_ref, l1b_ref, wup_hbm, fw_ref, fb_ref, wdn_hbm,
                    l2g_ref, l2b_ref, o_ref, ubuf_ref, act_ref,
                    wout_ref, wup_ref, wdn_ref, stage_up, sem_up, stage_d, sem_d, *, tile, d_ff):
    n_lane_chunks = 2 * d_ff // V7X_LANES

    @pl.when(pl.program_id(0) == 0)
    def _():
        ubuf_ref[:, 0:FFN_HIST, :] = jnp.zeros((n_lane_chunks, FFN_HIST, V7X_LANES), F32)
        _load_weight_bf16(wout_hbm, wout_ref, stage_d, sem_d)
        _load_weight_bf16(wup_hbm, wup_ref, stage_up, sem_up)
        _load_weight_bf16(wdn_hbm, wdn_ref, stage_d, sem_d)

    t = jnp.dot(m_ref[...], wout_ref[...], preferred_element_type=F32)
    z1 = DEEPNORM_ALPHA * x_ref[...] + mod_ref[2:3, :] * t
    x1 = _ln(z1) * l1g_ref[...] + l1b_ref[...]

    h = _ln(x1) * (1.0 + mod_ref[4:5, :]) + mod_ref[3:4, :]
    hb = h.astype(BF16)

    def conv_cols(base):
        u = jnp.dot(hb, wup_ref[:, base:base + FFN_COLS], preferred_element_type=F32)
        outs = []
        for jj in range(FFN_COLS // V7X_LANES):
            j = base // V7X_LANES + jj
            ls = slice(j * V7X_LANES, (j + 1) * V7X_LANES)
            uj = u[:, jj * V7X_LANES:(jj + 1) * V7X_LANES]
            ubuf_ref[j, FFN_HIST:FFN_HIST + tile, :] = uj
            y = fb_ref[:, ls] + fw_ref[2:3, ls] * uj
            for k in range(FFN_CONV_KERNEL - 1):
                r0 = FFN_HIST - (FFN_CONV_KERNEL - 1) + k
                y = y + fw_ref[k:k + 1, ls] * ubuf_ref[j, r0:r0 + tile, :]
            ubuf_ref[j, 0:FFN_HIST, :] = ubuf_ref[j, tile:tile + FFN_HIST, :]
            outs.append(y)
        return jnp.concatenate(outs, axis=1)

    for gi in range(d_ff // FFN_COLS):
        val = conv_cols(gi * FFN_COLS)
        gate = conv_cols(d_ff + gi * FFN_COLS)
        act_ref[:, gi * FFN_COLS:(gi + 1) * FFN_COLS] = (val * (gate * _sigmoid(gate))).astype(BF16)

    f = jnp.dot(act_ref[...], wdn_ref[...], preferred_element_type=F32)
    z2 = DEEPNORM_ALPHA * x1 + mod_ref[5:6, :] * f
    o_ref[...] = _ln(z2) * l2g_ref[...] + l2b_ref[...]


def _out_ffn(x2, m, mod, w_out, l1g, l1b, w_up, fw, fb, w_dn, l2g, l2b, tile):
    s_len, d = x2.shape
    d_ff = w_dn.shape[0]
    kern = functools.partial(_out_ffn_kernel, tile=tile, d_ff=d_ff)
    return pl.pallas_call(
        kern,
        grid=(s_len // tile,),
        in_specs=[
            _rows(tile, d),
            _rows(tile, d),
            _resident((N_MOD, d)),
            _HBM,
            _resident((1, d)),
            _resident((1, d)),
            _HBM,
            _resident(fw.shape),
            _resident((1, 2 * d_ff)),
            _HBM,
            _resident((1, d)),
            _resident((1, d)),
        ],
        out_specs=_rows(tile, d),
        out_shape=jax.ShapeDtypeStruct((s_len, d), F32),
        scratch_shapes=[
            pltpu.VMEM((2 * d_ff // V7X_LANES, FFN_HIST + tile, V7X_LANES), F32),
            pltpu.VMEM((tile, d_ff), BF16),
            pltpu.VMEM(w_out.shape, BF16),
            pltpu.VMEM(w_up.shape, BF16),
            pltpu.VMEM(w_dn.shape, BF16),
            *_stage_scratch(w_up.shape[1], FFN_STAGE_CHUNK_BYTES),
            *_stage_scratch(d, FFN_STAGE_CHUNK_BYTES),
        ],
        compiler_params=_params(),
        name="out_ffn",
    )(x2, m, mod, w_out, l1g, l1b, w_up, fw, fb, w_dn, l2g, l2b)


def kernel(x, c, positions, w_ada, b_ada, w_in, b_in, conv_dw_w, conv_dw_b, conv_ln_g, conv_ln_b,
           w_conv_out, ret_gn_g, ret_gn_b, w_ret_out, w_out, ln1_g, ln1_b, w_up, ffn_dw_w, ffn_dw_b,
           w_down, ln2_g, ln2_b):
    bsz, s_len, d = x.shape
    assert bsz == 1 and s_len % SEQ_TILE == 0 and d % V7X_LANES == 0
    depth = w_ada.shape[0]

    row = lambda v: v.reshape(1, -1)
    x2 = x.reshape(s_len, d)
    pos = positions.reshape(s_len, 1).astype(F32)
    for l in range(depth):
        mod = _adaln_mod(c, w_ada[l], b_ada[l])
        m = _token_branches(x2, pos, mod, w_in[l], row(b_in[l]), conv_dw_w[l], row(conv_dw_b[l]),
                            row(conv_ln_g[l]), row(conv_ln_b[l]), w_conv_out[l],
                            row(ret_gn_g[l]), row(ret_gn_b[l]), w_ret_out[l], SEQ_TILE)
        x2 = _out_ffn(x2, m, mod, w_out[l], row(ln1_g[l]), row(ln1_b[l]), w_up[l],
                      ffn_dw_w[l], row(ffn_dw_b[l]), w_down[l], row(ln2_g[l]), row(ln2_b[l]),
                      SEQ_TILE)
    return x2.reshape(bsz, s_len, d)
```

```python
import functools

import jax
import jax.numpy as jnp
import numpy as np
from jax import lax
from jax.experimental import pallas as pl
from jax.experimental.pallas import tpu as pltpu

F32 = jnp.float32
BF16 = jnp.bfloat16

DEPTH = 1
N_HEADS = 8
RET_CHUNK = 128
CONV_KERNEL = 31
FFN_CONV_KERNEL = 3
ROPE_BASE = 10000.0
LN_EPS = 1e-5
DEEPNORM_ALPHA = (2.0 * DEPTH) ** 0.25
N_MOD = 6

V7X_LANES = 128
V7X_SUBLANES = 8

SEQ_TILE = 512
CONV_HIST = 32
FFN_HIST = V7X_SUBLANES
CONV_ROWS = 128
FFN_COLS = 256
VMEM_LIMIT_BYTES = 63 * 1024 * 1024
TOKEN_STAGE_CHUNK_BYTES = 256 * 1024
FFN_STAGE_CHUNK_BYTES = 1024 * 1024
STAGE_SLOTS = 4
TOKEN_WIDE_STAGE_SLOTS = 6


def _ln(x):
    mu = jnp.mean(x, axis=-1, keepdims=True)
    xc = x - mu
    var = jnp.mean(xc * xc, axis=-1, keepdims=True)
    return xc * lax.rsqrt(var + LN_EPS)


def _sigmoid(x):
    return 0.5 * jnp.tanh(0.5 * x) + 0.5


def _resident(shape):
    nd = len(shape)
    return pl.BlockSpec(shape, lambda i: (0,) * nd, pipeline_mode=pl.Buffered(1))


def _rows(tile, width):
    return pl.BlockSpec((tile, width), lambda i: (i, 0))


def _params():
    return pltpu.CompilerParams(dimension_semantics=("arbitrary",), vmem_limit_bytes=VMEM_LIMIT_BYTES)


_HBM = pl.BlockSpec(memory_space=pl.ANY)


def _stage_scratch(width, chunk_bytes, slots=STAGE_SLOTS):
    rows = max(chunk_bytes // (width * 4), 2 * V7X_SUBLANES)
    rows -= rows % (2 * V7X_SUBLANES)
    return [pltpu.VMEM((slots, rows, width), F32), pltpu.SemaphoreType.DMA((slots,))]


def _load_weight_bf16(w_hbm, w_vmem, stage, sem):
    n_slots, rows = stage.shape[0], stage.shape[1]
    n_chunks = w_hbm.shape[0] // rows
    assert n_chunks * rows == w_hbm.shape[0] and stage.shape[2] == w_hbm.shape[1] and n_chunks >= n_slots

    def copy(c, slot):
        r0 = pl.multiple_of(c * rows, rows)
        return pltpu.make_async_copy(w_hbm.at[pl.ds(r0, rows), :], stage.at[slot], sem.at[slot])

    for c in range(n_slots - 1):
        copy(c, c).start()

    def body(c, carry):
        slot = c % n_slots
        ahead = c + n_slots - 1

        @pl.when(ahead < n_chunks)
        def _():
            copy(ahead, ahead % n_slots).start()

        copy(c, slot).wait()
        r0 = pl.multiple_of(c * rows, rows)
        w_vmem[pl.ds(r0, rows), :] = stage[slot].astype(BF16)
        return carry

    lax.fori_loop(0, n_chunks, body, 0)


def _mod_kernel(c_ref, w_ref, b_ref, o_ref):
    c = c_ref[...]
    s = c * _sigmoid(c)
    o_ref[...] = jnp.dot(s, w_ref[...], preferred_element_type=F32) + b_ref[...]


def _adaln_mod(c, w_ada, b_ada):
    d = c.shape[-1]
    c8 = jnp.broadcast_to(c, (V7X_SUBLANES, d))
    out = pl.pallas_call(
        _mod_kernel,
        grid=(N_MOD,),
        in_specs=[
            pl.BlockSpec((V7X_SUBLANES, d), lambda i: (0, 0)),
            pl.BlockSpec((d, d), lambda i: (0, i)),
            pl.BlockSpec((1, d), lambda i: (0, i)),
        ],
        out_specs=pl.BlockSpec((V7X_SUBLANES, d), lambda i: (0, i)),
        out_shape=jax.ShapeDtypeStruct((V7X_SUBLANES, N_MOD * d), F32),
        compiler_params=pltpu.CompilerParams(dimension_semantics=("arbitrary",)),
        name="adaln_mod",
    )(c8, w_ada, b_ada.reshape(1, -1))
    return out[0].reshape(N_MOD, d)


def _branches_kernel(x_ref, pos_ref, invf_ref, mod_ref, w_hbm, b_ref, cw_ref, cb_ref, clg_ref, clb_ref,
                     wco_hbm, gng_ref, gnb_ref, wro_hbm, decay_ref, xi_ref, zeta_ref, o_ref,
                     state_ref, abuf_ref, q_ref, qx_ref, k_ref, kz_ref, v_ref, sg_ref, ract_ref,
                     w_ref, wco_ref, wro_ref, stage_in, sem_in, stage_d, sem_d,
                     *, tile, d_model, dk, dv, chunk_decay):
    n_heads = d_model // dk
    v_w = n_heads * dv
    n_lane_chunks = d_model // V7X_LANES
    n_chunks = tile // RET_CHUNK
    o_q = 0
    o_k = o_q + d_model
    o_v = o_k + d_model
    o_g = o_v + v_w
    o_cv = o_g + v_w
    o_cg = o_cv + d_model
    o_ga = o_cg + d_model
    o_gb = o_ga + d_model

    @pl.when(pl.program_id(0) == 0)
    def _():
        state_ref[...] = jnp.zeros_like(state_ref)
        abuf_ref[:, 0:CONV_HIST, :] = jnp.zeros((n_lane_chunks, CONV_HIST, V7X_LANES), F32)
        _load_weight_bf16(w_hbm, w_ref, stage_in, sem_in)
        _load_weight_bf16(wco_hbm, wco_ref, stage_d, sem_d)
        _load_weight_bf16(wro_hbm, wro_ref, stage_d, sem_d)

    x = x_ref[...]
    h = _ln(x) * (1.0 + mod_ref[1:2, :]) + mod_ref[0:1, :]
    hb = h.astype(BF16)

    def proj(lo, width):
        return jnp.dot(hb, w_ref[:, lo:lo + width], preferred_element_type=F32) + b_ref[:, lo:lo + width]

    def anchor(v):
        bits = lax.bitcast_convert_type(v[0:1, 0:V7X_LANES].astype(F32), jnp.uint32)
        bits = lax.shift_right_logical(lax.shift_right_logical(bits, jnp.uint32(16)), jnp.uint32(16))
        return lax.bitcast_convert_type(bits, F32)

    a = proj(o_cv, d_model) * _sigmoid(proj(o_cg, d_model))
    for j in range(n_lane_chunks):
        abuf_ref[j, CONV_HIST:CONV_HIST + tile, :] = a[:, j * V7X_LANES:(j + 1) * V7X_LANES]
    first = CONV_HIST - (CONV_KERNEL - 1)

    def conv_chunk(j, zero):
        ls = slice(j * V7X_LANES, (j + 1) * V7X_LANES)
        bias = cb_ref[:, ls] if zero is None else cb_ref[:, ls] + zero
        blocks = []
        for rb in range(tile // CONV_ROWS):
            acc = jnp.broadcast_to(bias, (CONV_ROWS, V7X_LANES))
            for k in range(CONV_KERNEL):
                r0 = rb * CONV_ROWS + first + k
                acc = acc + cw_ref[k:k + 1, ls] * abuf_ref[j, r0:r0 + CONV_ROWS, :]
            blocks.append(acc)
        abuf_ref[j, 0:CONV_HIST, :] = abuf_ref[j, tile:tile + CONV_HIST, :]
        return jnp.concatenate(blocks, axis=0)

    half_t = tile // 2
    lane = lax.broadcasted_iota(jnp.int32, (half_t, dk), 1)
    low = lane < dk // 2
    ang = jnp.where(low, pos_ref[0:half_t, :], pos_ref[half_t:tile, :]) * invf_ref[...]
    cs_p, sn_p = jnp.cos(ang), jnp.sin(ang)
    cs_r, sn_r = pltpu.roll(cs_p, dk // 2, 1), pltpu.roll(sn_p, dk // 2, 1)
    cs = jnp.concatenate([jnp.where(low, cs_p, cs_r), jnp.where(low, cs_r, cs_p)], axis=0)
    sn = jnp.concatenate([jnp.where(low, -sn_p, sn_r), jnp.where(low, -sn_r, sn_p)], axis=0)
    k_scale = float(dk) ** -0.5

    def rope_heads(y, cos_t, sin_t):
        for hh in range(n_heads):
            yh = y[:, hh * dk:(hh + 1) * dk]
            yield hh, yh * cos_t + pltpu.roll(yh, dk // 2, 1) * sin_t

    def tiled(tab_ref, hh):
        return jnp.concatenate([tab_ref[:, hh * dk:(hh + 1) * dk]] * n_chunks, axis=0)

    half_v = v_w // 2
    vb0 = proj(o_v, half_v).astype(BF16)
    v_ref[:, 0:half_v] = vb0
    vb1 = proj(o_v + half_v, half_v).astype(BF16)
    v_ref[:, half_v:v_w] = vb1
    pins = {3: anchor(vb0), 6: anchor(vb1)}
    cols = [conv_chunk(j, pins.get(j)) for j in range(n_lane_chunks)]
    for hh, r in rope_heads(proj(o_q, d_model), cs, sn):
        q_ref[:, hh * dk:(hh + 1) * dk] = r.astype(BF16)
        qx_ref[:, hh * dk:(hh + 1) * dk] = (r * tiled(xi_ref, hh)).astype(BF16)
    for hh, r in rope_heads(proj(o_k, d_model), cs * k_scale, sn * k_scale):
        k_ref[:, hh * dk:(hh + 1) * dk] = r.astype(BF16)
        kz_ref[:, hh * dk:(hh + 1) * dk] = (r * tiled(zeta_ref, hh)).astype(BF16)
    g = proj(o_g, half_v)
    sg_ref[:, 0:half_v] = g * _sigmoid(g)
    g = proj(o_g + half_v, half_v)
    sg_ref[:, half_v:v_w] = g * _sigmoid(g)
    ga = _sigmoid(proj(o_ga, d_model))
    gb = _sigmoid(proj(o_gb, d_model))

    an = _ln(jnp.concatenate(cols, axis=1)) * clg_ref[...] + clb_ref[...]
    sa = (an * _sigmoid(an)).astype(BF16)
    ua = ga * jnp.dot(sa, wco_ref[...], preferred_element_type=F32)

    heads = range(n_heads)
    for ci in range(n_chunks):
        rows = slice(ci * RET_CHUNK, (ci + 1) * RET_CHUNK)
        qs = [slice(hh * dk, (hh + 1) * dk) for hh in heads]
        vs = [slice(hh * dv, (hh + 1) * dv) for hh in heads]
        sc = [lax.dot_general(q_ref[rows, qs[hh]], k_ref[rows, qs[hh]], (((1,), (1,)), ((), ())),
                              preferred_element_type=F32) for hh in heads]
        lhs = [jnp.concatenate([(sc[hh] * decay_ref[hh]).astype(BF16), qx_ref[rows, qs[hh]]], axis=1)
               for hh in heads]
        st = [state_ref[hh] for hh in heads]
        vh = [v_ref[rows, vs[hh]] for hh in heads]
        o = [jnp.dot(lhs[hh], jnp.concatenate([vh[hh], st[hh].astype(BF16)], axis=0),
                     preferred_element_type=F32) for hh in heads]
        kv = [lax.dot_general(kz_ref[rows, qs[hh]], vh[hh], (((0,), (0,)), ((), ())),
                              preferred_element_type=F32) for hh in heads]
        for hh in heads:
            state_ref[hh] = st[hh] * chunk_decay[hh] + kv[hh]
        for hh in heads:
            r = (_ln(o[hh]) * gng_ref[:, vs[hh]] + gnb_ref[:, vs[hh]]) * sg_ref[rows, vs[hh]]
            ract_ref[rows, vs[hh]] = r.astype(BF16)

    yb = jnp.dot(ract_ref[...], wro_ref[...], preferred_element_type=F32)
    o_ref[...] = (ua + gb * yb).astype(BF16)


def _token_branches(x2, pos, mod, w_in, b_in, cw, cb, clg, clb, w_co, gn_g, gn_b, w_ro, tile):
    s_len, d = x2.shape
    dk = d // N_HEADS
    dv = w_ro.shape[0] // N_HEADS
    log_gamma = np.log(1.0 - 2.0 ** (-5.0 - np.arange(N_HEADS, dtype=np.float64)))
    idx = np.arange(RET_CHUNK, dtype=np.float64)
    rel = idx[:, None] - idx[None, :]
    decay = np.where(rel[None] >= 0, np.exp(log_gamma[:, None, None] * np.maximum(rel, 0.0)[None]), 0.0)
    xi = np.repeat(np.exp(log_gamma[None, :] * (idx[:, None] + 1.0)), dk, axis=1)
    zeta = np.repeat(np.exp(log_gamma[None, :] * (RET_CHUNK - 1.0 - idx[:, None])), dk, axis=1)
    chunk_decay = tuple(float(v) for v in np.exp(log_gamma * RET_CHUNK))
    half = dk // 2
    inv_freq = ROPE_BASE ** (-jnp.arange(half, dtype=F32) / half)
    inv_freq = jnp.concatenate([inv_freq, inv_freq]).reshape(1, dk)

    kern = functools.partial(_branches_kernel, tile=tile, d_model=d, dk=dk, dv=dv, chunk_decay=chunk_decay)
    return pl.pallas_call(
        kern,
        grid=(s_len // tile,),
        in_specs=[
            _rows(tile, d),
            _rows(tile, 1),
            _resident((1, dk)),
            _resident((N_MOD, d)),
            _HBM,
            _resident(b_in.shape),
            _resident(cw.shape),
            _resident((1, d)),
            _resident((1, d)),
            _resident((1, d)),
            _HBM,
            _resident((1, N_HEADS * dv)),
            _resident((1, N_HEADS * dv)),
            _HBM,
            _resident((N_HEADS, RET_CHUNK, RET_CHUNK)),
            _resident((RET_CHUNK, d)),
            _resident((RET_CHUNK, d)),
        ],
        out_specs=_rows(tile, d),
        out_shape=jax.ShapeDtypeStruct((s_len, d), BF16),
        scratch_shapes=[
            pltpu.VMEM((N_HEADS, dk, dv), F32),
            pltpu.VMEM((d // V7X_LANES, CONV_HIST + tile, V7X_LANES), F32),
            pltpu.VMEM((tile, d), BF16),
            pltpu.VMEM((tile, d), BF16),
            pltpu.VMEM((tile, d), BF16),
            pltpu.VMEM((tile, d), BF16),
            pltpu.VMEM((tile, N_HEADS * dv), BF16),
            pltpu.VMEM((tile, N_HEADS * dv), F32),
            pltpu.VMEM((tile, N_HEADS * dv), BF16),
            pltpu.VMEM(w_in.shape, BF16),
            pltpu.VMEM(w_co.shape, BF16),
            pltpu.VMEM(w_ro.shape, BF16),
            *_stage_scratch(w_in.shape[1], TOKEN_STAGE_CHUNK_BYTES, slots=TOKEN_WIDE_STAGE_SLOTS),
            *_stage_scratch(d, 2 * TOKEN_STAGE_CHUNK_BYTES),
        ],
        compiler_params=_params(),
        name="token_branches",
    )(x2, pos, inv_freq, mod, w_in, b_in, cw, cb, clg, clb, w_co, gn_g, gn_b, w_ro,
      jnp.asarray(decay, F32), jnp.asarray(xi, F32), jnp.asarray(zeta, F32))


def _out_ffn_kernel(x_ref, m_ref, mod_ref, wout_hbm, l1g_ref, l1b_ref, wup_hbm, fw_ref, fb_ref, wdn_hbm,
                    l2g_ref, l2b_ref, o_ref, ubuf_ref, act_ref,
                    wout_ref, wup_ref, wdn_ref, stage_up, sem_up, stage_d, sem_d, *, tile, d_ff):
    n_lane_chunks = 2 * d_ff // V7X_LANES

    @pl.when(pl.program_id(0) == 0)
    def _():
        ubuf_ref[:, 0:FFN_HIST, :] = jnp.zeros((n_lane_chunks, FFN_HIST, V7X_LANES), F32)
        _load_weight_bf16(wout_hbm, wout_ref, stage_d, sem_d)
        _load_weight_bf16(wup_hbm, wup_ref, stage_up, sem_up)
        _load_weight_bf16(wdn_hbm, wdn_ref, stage_d, sem_d)

    t = jnp.dot(m_ref[...], wout_ref[...], preferred_element_type=F32)
    z1 = DEEPNORM_ALPHA * x_ref[...] + mod_ref[2:3, :] * t
    x1 = _ln(z1) * l1g_ref[...] + l1b_ref[...]

    h = _ln(x1) * (1.0 + mod_ref[4:5, :]) + mod_ref[3:4, :]
    hb = h.astype(BF16)

    def conv_cols(base):
        u = jnp.dot(hb, wup_ref[:, base:base + FFN_COLS], preferred_element_type=F32)
        outs = []
        for jj in range(FFN_COLS // V7X_LANES):
            j = base // V7X_LANES + jj
            ls = slice(j * V7X_LANES, (j + 1) * V7X_LANES)
            uj = u[:, jj * V7X_LANES:(jj + 1) * V7X_LANES]
            ubuf_ref[j, FFN_HIST:FFN_HIST + tile, :] = uj
            y = fb_ref[:, ls] + fw_ref[2:3, ls] * uj
            for k in range(FFN_CONV_KERNEL - 1):
                r0 = FFN_HIST - (FFN_CONV_KERNEL - 1) + k
                y = y + fw_ref[k:k + 1, ls] * ubuf_ref[j, r0:r0 + tile, :]
            ubuf_ref[j, 0:FFN_HIST, :] = ubuf_ref[j, tile:tile + FFN_HIST, :]
            outs.append(y)
        return jnp.concatenate(outs, axis=1)

    for gi in range(d_ff // FFN_COLS):
        val = conv_cols(gi * FFN_COLS)
        gate = conv_cols(d_ff + gi * FFN_COLS)
        act_ref[:, gi * FFN_COLS:(gi + 1) * FFN_COLS] = (val * (gate * _sigmoid(gate))).astype(BF16)

    f = jnp.dot(act_ref[...], wdn_ref[...], preferred_element_type=F32)
    z2 = DEEPNORM_ALPHA * x1 + mod_ref[5:6, :] * f
    o_ref[...] = _ln(z2) * l2g_ref[...] + l2b_ref[...]


def _out_ffn(x2, m, mod, w_out, l1g, l1b, w_up, fw, fb, w_dn, l2g, l2b, tile):
    s_len, d = x2.shape
    d_ff = w_dn.shape[0]
    kern = functools.partial(_out_ffn_kernel, tile=tile, d_ff=d_ff)
    return pl.pallas_call(
        kern,
        grid=(s_len // tile,),
        in_specs=[
            _rows(tile, d),
            _rows(tile, d),
            _resident((N_MOD, d)),
            _HBM,
            _resident((1, d)),
            _resident((1, d)),
            _HBM,
            _resident(fw.shape),
            _resident((1, 2 * d_ff)),
            _HBM,
            _resident((1, d)),
            _resident((1, d)),
        ],
        out_specs=_rows(tile, d),
        out_shape=jax.ShapeDtypeStruct((s_len, d), F32),
        scratch_shapes=[
            pltpu.VMEM((2 * d_ff // V7X_LANES, FFN_HIST + tile, V7X_LANES), F32),
            pltpu.VMEM((tile, d_ff), BF16),
            pltpu.VMEM(w_out.shape, BF16),
            pltpu.VMEM(w_up.shape, BF16),
            pltpu.VMEM(w_dn.shape, BF16),
            *_stage_scratch(w_up.shape[1], FFN_STAGE_CHUNK_BYTES),
            *_stage_scratch(d, FFN_STAGE_CHUNK_BYTES),
        ],
        compiler_params=_params(),
        name="out_ffn",
    )(x2, m, mod, w_out, l1g, l1b, w_up, fw, fb, w_dn, l2g, l2b)


def kernel(x, c, positions, w_ada, b_ada, w_in, b_in, conv_dw_w, conv_dw_b, conv_ln_g, conv_ln_b,
           w_conv_out, ret_gn_g, ret_gn_b, w_ret_out, w_out, ln1_g, ln1_b, w_up, ffn_dw_w, ffn_dw_b,
           w_down, ln2_g, ln2_b):
    bsz, s_len, d = x.shape
    assert bsz == 1 and s_len % SEQ_TILE == 0 and d % V7X_LANES == 0
    depth = w_ada.shape[0]

    row = lambda v: v.reshape(1, -1)
    x2 = x.reshape(s_len, d)
    pos = positions.reshape(s_len, 1).astype(F32)
    for l in range(depth):
        mod = _adaln_mod(c, w_ada[l], b_ada[l])
        m = _token_branches(x2, pos, mod, w_in[l], row(b_in[l]), conv_dw_w[l], row(conv_dw_b[l]),
                            row(conv_ln_g[l]), row(conv_ln_b[l]), w_conv_out[l],
                            row(ret_gn_g[l]), row(ret_gn_b[l]), w_ret_out[l], SEQ_TILE)
        x2 = _out_ffn(x2, m, mod, w_out[l], row(ln1_g[l]), row(ln1_b[l]), w_up[l],
                      ffn_dw_w[l], row(ffn_dw_b[l]), w_down[l], row(ln2_g[l]), row(ln2_b[l]),
                      SEQ_TILE)
    return x2.reshape(bsz, s_len, d)
```

```python
import functools

import jax
import jax.numpy as jnp
import numpy as np
from jax import lax
from jax.experimental import pallas as pl
from jax.experimental.pallas import tpu as pltpu

F32 = jnp.float32
BF16 = jnp.bfloat16

DEPTH = 1
N_HEADS = 8
RET_CHUNK = 128
CONV_KERNEL = 31
FFN_CONV_KERNEL = 3
ROPE_BASE = 10000.0
LN_EPS = 1e-5
DEEPNORM_ALPHA = (2.0 * DEPTH) ** 0.25
N_MOD = 6

V7X_LANES = 128
V7X_SUBLANES = 8

SEQ_TILE = 512
CONV_HIST = 32
FFN_HIST = V7X_SUBLANES
CONV_ROWS = 128
FFN_COLS = 256
VMEM_LIMIT_BYTES = 63 * 1024 * 1024
TOKEN_STAGE_CHUNK_BYTES = 256 * 1024
FFN_STAGE_CHUNK_BYTES = 1024 * 1024
STAGE_SLOTS = 4
WIDE_STAGE_SLOTS = 6


def _ln(x):
    mu = jnp.mean(x, axis=-1, keepdims=True)
    xc = x - mu
    var = jnp.mean(xc * xc, axis=-1, keepdims=True)
    return xc * lax.rsqrt(var + LN_EPS)


def _sigmoid(x):
    return 0.5 * jnp.tanh(0.5 * x) + 0.5


def _resident(shape):
    nd = len(shape)
    return pl.BlockSpec(shape, lambda i: (0,) * nd, pipeline_mode=pl.Buffered(1))


def _rows(tile, width):
    return pl.BlockSpec((tile, width), lambda i: (i, 0))


def _params():
    return pltpu.CompilerParams(dimension_semantics=("arbitrary",), vmem_limit_bytes=VMEM_LIMIT_BYTES)


_HBM = pl.BlockSpec(memory_space=pl.ANY)


def _stage_scratch(width, chunk_bytes, slots=STAGE_SLOTS):
    rows = max(chunk_bytes // (width * 4), 2 * V7X_SUBLANES)
    rows -= rows % (2 * V7X_SUBLANES)
    return [pltpu.VMEM((slots, rows, width), F32), pltpu.SemaphoreType.DMA((slots,))]


def _load_weight_bf16(w_hbm, w_vmem, stage, sem):
    n_slots, rows = stage.shape[0], stage.shape[1]
    n_chunks = w_hbm.shape[0] // rows
    assert n_chunks * rows == w_hbm.shape[0] and stage.shape[2] == w_hbm.shape[1] and n_chunks >= n_slots

    def copy(c, slot):
        r0 = pl.multiple_of(c * rows, rows)
        return pltpu.make_async_copy(w_hbm.at[pl.ds(r0, rows), :], stage.at[slot], sem.at[slot])

    for c in range(n_slots - 1):
        copy(c, c).start()

    def body(c, carry):
        slot = c % n_slots
        ahead = c + n_slots - 1

        @pl.when(ahead < n_chunks)
        def _():
            copy(ahead, ahead % n_slots).start()

        copy(c, slot).wait()
        r0 = pl.multiple_of(c * rows, rows)
        w_vmem[pl.ds(r0, rows), :] = stage[slot].astype(BF16)
        return carry

    lax.fori_loop(0, n_chunks, body, 0)


def _mod_kernel(c_ref, w_ref, b_ref, o_ref):
    c = c_ref[...]
    s = c * _sigmoid(c)
    o_ref[...] = jnp.dot(s, w_ref[...], preferred_element_type=F32) + b_ref[...]


def _adaln_mod(c, w_ada, b_ada):
    d = c.shape[-1]
    c8 = jnp.broadcast_to(c, (V7X_SUBLANES, d))
    out = pl.pallas_call(
        _mod_kernel,
        grid=(N_MOD,),
        in_specs=[
            pl.BlockSpec((V7X_SUBLANES, d), lambda i: (0, 0)),
            pl.BlockSpec((d, d), lambda i: (0, i)),
            pl.BlockSpec((1, d), lambda i: (0, i)),
        ],
        out_specs=pl.BlockSpec((V7X_SUBLANES, d), lambda i: (0, i)),
        out_shape=jax.ShapeDtypeStruct((V7X_SUBLANES, N_MOD * d), F32),
        compiler_params=pltpu.CompilerParams(dimension_semantics=("arbitrary",)),
        name="adaln_mod",
    )(c8, w_ada, b_ada.reshape(1, -1))
    return out[0].reshape(N_MOD, d)


def _branches_kernel(x_ref, pos_ref, invf_ref, mod_ref, w_hbm, b_ref, cw_ref, cb_ref, clg_ref, clb_ref,
                     wco_hbm, gng_ref, gnb_ref, wro_hbm, decay_ref, xi_ref, zeta_ref, o_ref,
                     state_ref, abuf_ref, q_ref, qx_ref, k_ref, kz_ref, v_ref, sg_ref, ract_ref,
                     w_ref, wco_ref, wro_ref, stage_in, sem_in, stage_d, sem_d,
                     *, tile, d_model, dk, dv, chunk_decay):
    n_heads = d_model // dk
    v_w = n_heads * dv
    n_lane_chunks = d_model // V7X_LANES
    n_chunks = tile // RET_CHUNK
    o_q = 0
    o_k = o_q + d_model
    o_v = o_k + d_model
    o_g = o_v + v_w
    o_cv = o_g + v_w
    o_cg = o_cv + d_model
    o_ga = o_cg + d_model
    o_gb = o_ga + d_model

    @pl.when(pl.program_id(0) == 0)
    def _():
        state_ref[...] = jnp.zeros_like(state_ref)
        abuf_ref[:, 0:CONV_HIST, :] = jnp.zeros((n_lane_chunks, CONV_HIST, V7X_LANES), F32)
        _load_weight_bf16(w_hbm, w_ref, stage_in, sem_in)
        _load_weight_bf16(wco_hbm, wco_ref, stage_d, sem_d)
        _load_weight_bf16(wro_hbm, wro_ref, stage_d, sem_d)

    x = x_ref[...]
    h = _ln(x) * (1.0 + mod_ref[1:2, :]) + mod_ref[0:1, :]
    hb = h.astype(BF16)

    def proj(lo, width):
        return jnp.dot(hb, w_ref[:, lo:lo + width], preferred_element_type=F32) + b_ref[:, lo:lo + width]

    def anchor(v):
        bits = lax.bitcast_convert_type(v[0:1, 0:V7X_LANES].astype(F32), jnp.uint32)
        bits = lax.shift_right_logical(lax.shift_right_logical(bits, jnp.uint32(16)), jnp.uint32(16))
        return lax.bitcast_convert_type(bits, F32)

    a = proj(o_cv, d_model) * _sigmoid(proj(o_cg, d_model))
    for j in range(n_lane_chunks):
        abuf_ref[j, CONV_HIST:CONV_HIST + tile, :] = a[:, j * V7X_LANES:(j + 1) * V7X_LANES]
    first = CONV_HIST - (CONV_KERNEL - 1)

    def conv_chunk(j, zero):
        ls = slice(j * V7X_LANES, (j + 1) * V7X_LANES)
        bias = cb_ref[:, ls] if zero is None else cb_ref[:, ls] + zero
        blocks = []
        for rb in range(tile // CONV_ROWS):
            acc = jnp.broadcast_to(bias, (CONV_ROWS, V7X_LANES))
            for k in range(CONV_KERNEL):
                r0 = rb * CONV_ROWS + first + k
                acc = acc + cw_ref[k:k + 1, ls] * abuf_ref[j, r0:r0 + CONV_ROWS, :]
            blocks.append(acc)
        abuf_ref[j, 0:CONV_HIST, :] = abuf_ref[j, tile:tile + CONV_HIST, :]
        return jnp.concatenate(blocks, axis=0)

    half_t = tile // 2
    lane = lax.broadcasted_iota(jnp.int32, (half_t, dk), 1)
    low = lane < dk // 2
    ang = jnp.where(low, pos_ref[0:half_t, :], pos_ref[half_t:tile, :]) * invf_ref[...]
    cs_p, sn_p = jnp.cos(ang), jnp.sin(ang)
    cs_r, sn_r = pltpu.roll(cs_p, dk // 2, 1), pltpu.roll(sn_p, dk // 2, 1)
    cs = jnp.concatenate([jnp.where(low, cs_p, cs_r), jnp.where(low, cs_r, cs_p)], axis=0)
    sn = jnp.concatenate([jnp.where(low, -sn_p, sn_r), jnp.where(low, -sn_r, sn_p)], axis=0)
    k_scale = float(dk) ** -0.5

    def rope_heads(y, cos_t, sin_t):
        for hh in range(n_heads):
            yh = y[:, hh * dk:(hh + 1) * dk]
            yield hh, yh * cos_t + pltpu.roll(yh, dk // 2, 1) * sin_t

    def tiled(tab_ref, hh):
        return jnp.concatenate([tab_ref[:, hh * dk:(hh + 1) * dk]] * n_chunks, axis=0)

    half_v = v_w // 2
    vb0 = proj(o_v, half_v).astype(BF16)
    v_ref[:, 0:half_v] = vb0
    vb1 = proj(o_v + half_v, half_v).astype(BF16)
    v_ref[:, half_v:v_w] = vb1
    pins = {3: anchor(vb0), 6: anchor(vb1)}
    cols = [conv_chunk(j, pins.get(j)) for j in range(n_lane_chunks)]
    for hh, r in rope_heads(proj(o_q, d_model), cs, sn):
        q_ref[:, hh * dk:(hh + 1) * dk] = r.astype(BF16)
        qx_ref[:, hh * dk:(hh + 1) * dk] = (r * tiled(xi_ref, hh)).astype(BF16)
    for hh, r in rope_heads(proj(o_k, d_model), cs * k_scale, sn * k_scale):
        k_ref[:, hh * dk:(hh + 1) * dk] = r.astype(BF16)
        kz_ref[:, hh * dk:(hh + 1) * dk] = (r * tiled(zeta_ref, hh)).astype(BF16)
    g = proj(o_g, half_v)
    sg_ref[:, 0:half_v] = g * _sigmoid(g)
    g = proj(o_g + half_v, half_v)
    sg_ref[:, half_v:v_w] = g * _sigmoid(g)
    ga = _sigmoid(proj(o_ga, d_model))
    gb = _sigmoid(proj(o_gb, d_model))

    an = _ln(jnp.concatenate(cols, axis=1)) * clg_ref[...] + clb_ref[...]
    sa = (an * _sigmoid(an)).astype(BF16)
    ua = ga * jnp.dot(sa, wco_ref[...], preferred_element_type=F32)

    heads = range(n_heads)
    for ci in range(n_chunks):
        rows = slice(ci * RET_CHUNK, (ci + 1) * RET_CHUNK)
        qs = [slice(hh * dk, (hh + 1) * dk) for hh in heads]
        vs = [slice(hh * dv, (hh + 1) * dv) for hh in heads]
        sc = [lax.dot_general(q_ref[rows, qs[hh]], k_ref[rows, qs[hh]], (((1,), (1,)), ((), ())),
                              preferred_element_type=F32) for hh in heads]
        lhs = [jnp.concatenate([(sc[hh] * decay_ref[hh]).astype(BF16), qx_ref[rows, qs[hh]]], axis=1)
               for hh in heads]
        st = [state_ref[hh] for hh in heads]
        vh = [v_ref[rows, vs[hh]] for hh in heads]
        o = [jnp.dot(lhs[hh], jnp.concatenate([vh[hh], st[hh].astype(BF16)], axis=0),
                     preferred_element_type=F32) for hh in heads]
        kv = [lax.dot_general(kz_ref[rows, qs[hh]], vh[hh], (((0,), (0,)), ((), ())),
                              preferred_element_type=F32) for hh in heads]
        for hh in heads:
            state_ref[hh] = st[hh] * chunk_decay[hh] + kv[hh]
        for hh in heads:
            r = (_ln(o[hh]) * gng_ref[:, vs[hh]] + gnb_ref[:, vs[hh]]) * sg_ref[rows, vs[hh]]
            ract_ref[rows, vs[hh]] = r.astype(BF16)

    yb = jnp.dot(ract_ref[...], wro_ref[...], preferred_element_type=F32)
    o_ref[...] = (ua + gb * yb).astype(BF16)


def _token_branches(x2, pos, mod, w_in, b_in, cw, cb, clg, clb, w_co, gn_g, gn_b, w_ro, tile):
    s_len, d = x2.shape
    dk = d // N_HEADS
    dv = w_ro.shape[0] // N_HEADS
    log_gamma = np.log(1.0 - 2.0 ** (-5.0 - np.arange(N_HEADS, dtype=np.float64)))
    idx = np.arange(RET_CHUNK, dtype=np.float64)
    rel = idx[:, None] - idx[None, :]
    decay = np.where(rel[None] >= 0, np.exp(log_gamma[:, None, None] * np.maximum(rel, 0.0)[None]), 0.0)
    xi = np.repeat(np.exp(log_gamma[None, :] * (idx[:, None] + 1.0)), dk, axis=1)
    zeta = np.repeat(np.exp(log_gamma[None, :] * (RET_CHUNK - 1.0 - idx[:, None])), dk, axis=1)
    chunk_decay = tuple(float(v) for v in np.exp(log_gamma * RET_CHUNK))
    half = dk // 2
    inv_freq = ROPE_BASE ** (-jnp.arange(half, dtype=F32) / half)
    inv_freq = jnp.concatenate([inv_freq, inv_freq]).reshape(1, dk)

    kern = functools.partial(_branches_kernel, tile=tile, d_model=d, dk=dk, dv=dv, chunk_decay=chunk_decay)
    return pl.pallas_call(
        kern,
        grid=(s_len // tile,),
        in_specs=[
            _rows(tile, d),
            _rows(tile, 1),
            _resident((1, dk)),
            _resident((N_MOD, d)),
            _HBM,
            _resident(b_in.shape),
            _resident(cw.shape),
            _resident((1, d)),
            _resident((1, d)),
            _resident((1, d)),
            _HBM,
            _resident((1, N_HEADS * dv)),
            _resident((1, N_HEADS * dv)),
            _HBM,
            _resident((N_HEADS, RET_CHUNK, RET_CHUNK)),
            _resident((RET_CHUNK, d)),
            _resident((RET_CHUNK, d)),
        ],
        out_specs=_rows(tile, d),
        out_shape=jax.ShapeDtypeStruct((s_len, d), BF16),
        scratch_shapes=[
            pltpu.VMEM((N_HEADS, dk, dv), F32),
            pltpu.VMEM((d // V7X_LANES, CONV_HIST + tile, V7X_LANES), F32),
            pltpu.VMEM((tile, d), BF16),
            pltpu.VMEM((tile, d), BF16),
            pltpu.VMEM((tile, d), BF16),
            pltpu.VMEM((tile, d), BF16),
            pltpu.VMEM((tile, N_HEADS * dv), BF16),
            pltpu.VMEM((tile, N_HEADS * dv), F32),
            pltpu.VMEM((tile, N_HEADS * dv), BF16),
            pltpu.VMEM(w_in.shape, BF16),
            pltpu.VMEM(w_co.shape, BF16),
            pltpu.VMEM(w_ro.shape, BF16),
            *_stage_scratch(w_in.shape[1], TOKEN_STAGE_CHUNK_BYTES, slots=WIDE_STAGE_SLOTS),
            *_stage_scratch(d, 2 * TOKEN_STAGE_CHUNK_BYTES),
        ],
        compiler_params=_params(),
        name="token_branches",
    )(x2, pos, inv_freq, mod, w_in, b_in, cw, cb, clg, clb, w_co, gn_g, gn_b, w_ro,
      jnp.asarray(decay, F32), jnp.asarray(xi, F32), jnp.asarray(zeta, F32))


def _out_ffn_kernel(x_ref, m_ref, mod_ref, wout_hbm, l1g_ref, l1b_ref, wup_hbm, fw_ref, fb_ref, wdn_hbm,
                    l2g_ref, l2b_ref, o_ref, ubuf_ref, act_ref,
                    wout_ref, wup_ref, wdn_ref, stage_up, sem_up, stage_d, sem_d, *, tile, d_ff):
    n_lane_chunks = 2 * d_ff // V7X_LANES

    @pl.when(pl.program_id(0) == 0)
    def _():
        ubuf_ref[:, 0:FFN_HIST, :] = jnp.zeros((n_lane_chunks, FFN_HIST, V7X_LANES), F32)
        _load_weight_bf16(wout_hbm, wout_ref, stage_d, sem_d)
        _load_weight_bf16(wup_hbm, wup_ref, stage_up, sem_up)
        _load_weight_bf16(wdn_hbm, wdn_ref, stage_d, sem_d)

    t = jnp.dot(m_ref[...], wout_ref[...], preferred_element_type=F32)
    z1 = DEEPNORM_ALPHA * x_ref[...] + mod_ref[2:3, :] * t
    x1 = _ln(z1) * l1g_ref[...] + l1b_ref[...]

    h = _ln(x1) * (1.0 + mod_ref[4:5, :]) + mod_ref[3:4, :]
    hb = h.astype(BF16)

    def conv_cols(base):
        u = jnp.dot(hb, wup_ref[:, base:base + FFN_COLS], preferred_element_type=F32)
        outs = []
        for jj in range(FFN_COLS // V7X_LANES):
            j = base // V7X_LANES + jj
            ls = slice(j * V7X_LANES, (j + 1) * V7X_LANES)
            uj = u[:, jj * V7X_LANES:(jj + 1) * V7X_LANES]
            ubuf_ref[j, FFN_HIST:FFN_HIST + tile, :] = uj
            y = fb_ref[:, ls] + fw_ref[2:3, ls] * uj
            for k in range(FFN_CONV_KERNEL - 1):
                r0 = FFN_HIST - (FFN_CONV_KERNEL - 1) + k
                y = y + fw_ref[k:k + 1, ls] * ubuf_ref[j, r0:r0 + tile, :]
            ubuf_ref[j, 0:FFN_HIST, :] = ubuf_ref[j, tile:tile + FFN_HIST, :]
            outs.append(y)
        return jnp.concatenate(outs, axis=1)

    for gi in range(d_ff // FFN_COLS):
        val = conv_cols(gi * FFN_COLS)
        gate = conv_cols(d_ff + gi * FFN_COLS)
        act_ref[:, gi * FFN_COLS:(gi + 1) * FFN_COLS] = (val * (gate * _sigmoid(gate))).astype(BF16)

    f = jnp.dot(act_ref[...], wdn_ref[...], preferred_element_type=F32)
    z2 = DEEPNORM_ALPHA * x1 + mod_ref[5:6, :] * f
    o_ref[...] = _ln(z2) * l2g_ref[...] + l2b_ref[...]


def _out_ffn(x2, m, mod, w_out, l1g, l1b, w_up, fw, fb, w_dn, l2g, l2b, tile):
    s_len, d = x2.shape
    d_ff = w_dn.shape[0]
    kern = functools.partial(_out_ffn_kernel, tile=tile, d_ff=d_ff)
    return pl.pallas_call(
        kern,
        grid=(s_len // tile,),
        in_specs=[
            _rows(tile, d),
            _rows(tile, d),
            _resident((N_MOD, d)),
            _HBM,
            _resident((1, d)),
            _resident((1, d)),
            _HBM,
            _resident(fw.shape),
            _resident((1, 2 * d_ff)),
            _HBM,
            _resident((1, d)),
            _resident((1, d)),
        ],
        out_specs=_rows(tile, d),
        out_shape=jax.ShapeDtypeStruct((s_len, d), F32),
        scratch_shapes=[
            pltpu.VMEM((2 * d_ff // V7X_LANES, FFN_HIST + tile, V7X_LANES), F32),
            pltpu.VMEM((tile, d_ff), BF16),
            pltpu.VMEM(w_out.shape, BF16),
            pltpu.VMEM(w_up.shape, BF16),
            pltpu.VMEM(w_dn.shape, BF16),
            *_stage_scratch(w_up.shape[1], FFN_STAGE_CHUNK_BYTES, slots=WIDE_STAGE_SLOTS),
            *_stage_scratch(d, FFN_STAGE_CHUNK_BYTES),
        ],
        compiler_params=_params(),
        name="out_ffn",
    )(x2, m, mod, w_out, l1g, l1b, w_up, fw, fb, w_dn, l2g, l2b)


def kernel(x, c, positions, w_ada, b_ada, w_in, b_in, conv_dw_w, conv_dw_b, conv_ln_g, conv_ln_b,
           w_conv_out, ret_gn_g, ret_gn_b, w_ret_out, w_out, ln1_g, ln1_b, w_up, ffn_dw_w, ffn_dw_b,
           w_down, ln2_g, ln2_b):
    bsz, s_len, d = x.shape
    assert bsz == 1 and s_len % SEQ_TILE == 0 and d % V7X_LANES == 0
    depth = w_ada.shape[0]

    row = lambda v: v.reshape(1, -1)
    x2 = x.reshape(s_len, d)
    pos = positions.reshape(s_len, 1).astype(F32)
    for l in range(depth):
        mod = _adaln_mod(c, w_ada[l], b_ada[l])
        m = _token_branches(x2, pos, mod, w_in[l], row(b_in[l]), conv_dw_w[l], row(conv_dw_b[l]),
                            row(conv_ln_g[l]), row(conv_ln_b[l]), w_conv_out[l],
                            row(ret_gn_g[l]), row(ret_gn_b[l]), w_ret_out[l], SEQ_TILE)
        x2 = _out_ffn(x2, m, mod, w_out[l], row(ln1_g[l]), row(ln1_b[l]), w_up[l],
                      ffn_dw_w[l], row(ffn_dw_b[l]), w_down[l], row(ln2_g[l]), row(ln2_b[l]),
                      SEQ_TILE)
    return x2.reshape(bsz, s_len, d)
```

```python
import functools

import jax
import jax.numpy as jnp
import numpy as np
from jax import lax
from jax.experimental import pallas as pl
from jax.experimental.pallas import tpu as pltpu

F32 = jnp.float32
BF16 = jnp.bfloat16

DEPTH = 1
N_HEADS = 8
RET_CHUNK = 128
CONV_KERNEL = 31
FFN_CONV_KERNEL = 3
ROPE_BASE = 10000.0
LN_EPS = 1e-5
DEEPNORM_ALPHA = (2.0 * DEPTH) ** 0.25
N_MOD = 6

V7X_LANES = 128
V7X_SUBLANES = 8

SEQ_TILE = 512
CONV_HIST = 32
FFN_HIST = V7X_SUBLANES
CONV_ROWS = 128
FFN_COLS = 256
VMEM_LIMIT_BYTES = 63 * 1024 * 1024
TOKEN_STAGE_CHUNK_BYTES = 256 * 1024
FFN_STAGE_CHUNK_BYTES = 1024 * 1024
STAGE_SLOTS = 4
WIDE_STAGE_SLOTS = 6


def _ln(x):
    mu = jnp.mean(x, axis=-1, keepdims=True)
    xc = x - mu
    var = jnp.mean(xc * xc, axis=-1, keepdims=True)
    return xc * lax.rsqrt(var + LN_EPS)


def _sigmoid(x):
    return 0.5 * jnp.tanh(0.5 * x) + 0.5


def _resident(shape):
    nd = len(shape)
    return pl.BlockSpec(shape, lambda i: (0,) * nd, pipeline_mode=pl.Buffered(1))


def _rows(tile, width):
    return pl.BlockSpec((tile, width), lambda i: (i, 0))


def _params():
    return pltpu.CompilerParams(dimension_semantics=("arbitrary",), vmem_limit_bytes=VMEM_LIMIT_BYTES)


_HBM = pl.BlockSpec(memory_space=pl.ANY)


def _stage_scratch(width, chunk_bytes, slots=STAGE_SLOTS):
    rows = max(chunk_bytes // (width * 4), 2 * V7X_SUBLANES)
    rows -= rows % (2 * V7X_SUBLANES)
    return [pltpu.VMEM((slots, rows, width), F32), pltpu.SemaphoreType.DMA((slots,))]


def _load_weight_bf16(w_hbm, w_vmem, stage, sem):
    n_slots, rows = stage.shape[0], stage.shape[1]
    n_chunks = w_hbm.shape[0] // rows
    assert n_chunks * rows == w_hbm.shape[0] and stage.shape[2] == w_hbm.shape[1] and n_chunks >= n_slots

    def copy(c, slot):
        r0 = pl.multiple_of(c * rows, rows)
        return pltpu.make_async_copy(w_hbm.at[pl.ds(r0, rows), :], stage.at[slot], sem.at[slot])

    assert n_chunks % 2 == 0 and n_slots % 2 == 0
    for c in range(n_slots - 1):
        copy(c, c).start(priority=c % 2)

    def body(i, carry):
        for parity in range(2):
            c = 2 * i + parity
            slot = c % n_slots
            ahead = c + n_slots - 1

            @pl.when(ahead < n_chunks)
            def _():
                copy(ahead, ahead % n_slots).start(priority=(parity + n_slots - 1) % 2)

            copy(c, slot).wait()
            r0 = pl.multiple_of(c * rows, rows)
            w_vmem[pl.ds(r0, rows), :] = stage[slot].astype(BF16)
        return carry

    lax.fori_loop(0, n_chunks // 2, body, 0)


def _mod_kernel(c_ref, w_ref, b_ref, o_ref):
    c = c_ref[...]
    s = c * _sigmoid(c)
    o_ref[...] = jnp.dot(s, w_ref[...], preferred_element_type=F32) + b_ref[...]


def _adaln_mod(c, w_ada, b_ada):
    d = c.shape[-1]
    c8 = jnp.broadcast_to(c, (V7X_SUBLANES, d))
    out = pl.pallas_call(
        _mod_kernel,
        grid=(N_MOD,),
        in_specs=[
            pl.BlockSpec((V7X_SUBLANES, d), lambda i: (0, 0)),
            pl.BlockSpec((d, d), lambda i: (0, i)),
            pl.BlockSpec((1, d), lambda i: (0, i)),
        ],
        out_specs=pl.BlockSpec((V7X_SUBLANES, d), lambda i: (0, i)),
        out_shape=jax.ShapeDtypeStruct((V7X_SUBLANES, N_MOD * d), F32),
        compiler_params=pltpu.CompilerParams(dimension_semantics=("arbitrary",)),
        name="adaln_mod",
    )(c8, w_ada, b_ada.reshape(1, -1))
    return out[0].reshape(N_MOD, d)


def _branches_kernel(x_ref, pos_ref, invf_ref, mod_ref, w_hbm, b_ref, cw_ref, cb_ref, clg_ref, clb_ref,
                     wco_hbm, gng_ref, gnb_ref, wro_hbm, decay_ref, xi_ref, zeta_ref, o_ref,
                     state_ref, abuf_ref, q_ref, qx_ref, k_ref, kz_ref, v_ref, sg_ref, ract_ref,
                     w_ref, wco_ref, wro_ref, stage_in, sem_in, stage_d, sem_d,
                     *, tile, d_model, dk, dv, chunk_decay):
    n_heads = d_model // dk
    v_w = n_heads * dv
    n_lane_chunks = d_model // V7X_LANES
    n_chunks = tile // RET_CHUNK
    o_q = 0
    o_k = o_q + d_model
    o_v = o_k + d_model
    o_g = o_v + v_w
    o_cv = o_g + v_w
    o_cg = o_cv + d_model
    o_ga = o_cg + d_model
    o_gb = o_ga + d_model

    @pl.when(pl.program_id(0) == 0)
    def _():
        state_ref[...] = jnp.zeros_like(state_ref)
        abuf_ref[:, 0:CONV_HIST, :] = jnp.zeros((n_lane_chunks, CONV_HIST, V7X_LANES), F32)
        _load_weight_bf16(w_hbm, w_ref, stage_in, sem_in)
        _load_weight_bf16(wco_hbm, wco_ref, stage_d, sem_d)
        _load_weight_bf16(wro_hbm, wro_ref, stage_d, sem_d)

    x = x_ref[...]
    h = _ln(x) * (1.0 + mod_ref[1:2, :]) + mod_ref[0:1, :]
    hb = h.astype(BF16)

    def proj(lo, width):
        return jnp.dot(hb, w_ref[:, lo:lo + width], preferred_element_type=F32) + b_ref[:, lo:lo + width]

    def anchor(v):
        bits = lax.bitcast_convert_type(v[0:1, 0:V7X_LANES].astype(F32), jnp.uint32)
        bits = lax.shift_right_logical(lax.shift_right_logical(bits, jnp.uint32(16)), jnp.uint32(16))
        return lax.bitcast_convert_type(bits, F32)

    a = proj(o_cv, d_model) * _sigmoid(proj(o_cg, d_model))
    for j in range(n_lane_chunks):
        abuf_ref[j, CONV_HIST:CONV_HIST + tile, :] = a[:, j * V7X_LANES:(j + 1) * V7X_LANES]
    first = CONV_HIST - (CONV_KERNEL - 1)

    def conv_chunk(j, zero):
        ls = slice(j * V7X_LANES, (j + 1) * V7X_LANES)
        bias = cb_ref[:, ls] if zero is None else cb_ref[:, ls] + zero
        blocks = []
        for rb in range(tile // CONV_ROWS):
            acc = jnp.broadcast_to(bias, (CONV_ROWS, V7X_LANES))
            for k in range(CONV_KERNEL):
                r0 = rb * CONV_ROWS + first + k
                acc = acc + cw_ref[k:k + 1, ls] * abuf_ref[j, r0:r0 + CONV_ROWS, :]
            blocks.append(acc)
        abuf_ref[j, 0:CONV_HIST, :] = abuf_ref[j, tile:tile + CONV_HIST, :]
        return jnp.concatenate(blocks, axis=0)

    half_t = tile // 2
    lane = lax.broadcasted_iota(jnp.int32, (half_t, dk), 1)
    low = lane < dk // 2
    ang = jnp.where(low, pos_ref[0:half_t, :], pos_ref[half_t:tile, :]) * invf_ref[...]
    cs_p, sn_p = jnp.cos(ang), jnp.sin(ang)
    cs_r, sn_r = pltpu.roll(cs_p, dk // 2, 1), pltpu.roll(sn_p, dk // 2, 1)
    cs = jnp.concatenate([jnp.where(low, cs_p, cs_r), jnp.where(low, cs_r, cs_p)], axis=0)
    sn = jnp.concatenate([jnp.where(low, -sn_p, sn_r), jnp.where(low, -sn_r, sn_p)], axis=0)
    k_scale = float(dk) ** -0.5

    def rope_heads(y, cos_t, sin_t):
        for hh in range(n_heads):
            yh = y[:, hh * dk:(hh + 1) * dk]
            yield hh, yh * cos_t + pltpu.roll(yh, dk // 2, 1) * sin_t

    def tiled(tab_ref, hh):
        return jnp.concatenate([tab_ref[:, hh * dk:(hh + 1) * dk]] * n_chunks, axis=0)

    half_v = v_w // 2
    vb0 = proj(o_v, half_v).astype(BF16)
    v_ref[:, 0:half_v] = vb0
    vb1 = proj(o_v + half_v, half_v).astype(BF16)
    v_ref[:, half_v:v_w] = vb1
    pins = {3: anchor(vb0), 6: anchor(vb1)}
    cols = [conv_chunk(j, pins.get(j)) for j in range(n_lane_chunks)]
    for hh, r in rope_heads(proj(o_q, d_model), cs, sn):
        q_ref[:, hh * dk:(hh + 1) * dk] = r.astype(BF16)
        qx_ref[:, hh * dk:(hh + 1) * dk] = (r * tiled(xi_ref, hh)).astype(BF16)
    for hh, r in rope_heads(proj(o_k, d_model), cs * k_scale, sn * k_scale):
        k_ref[:, hh * dk:(hh + 1) * dk] = r.astype(BF16)
        kz_ref[:, hh * dk:(hh + 1) * dk] = (r * tiled(zeta_ref, hh)).astype(BF16)
    g = proj(o_g, half_v)
    sg_ref[:, 0:half_v] = g * _sigmoid(g)
    g = proj(o_g + half_v, half_v)
    sg_ref[:, half_v:v_w] = g * _sigmoid(g)
    ga = _sigmoid(proj(o_ga, d_model))
    gb = _sigmoid(proj(o_gb, d_model))

    an = _ln(jnp.concatenate(cols, axis=1)) * clg_ref[...] + clb_ref[...]
    sa = (an * _sigmoid(an)).astype(BF16)
    ua = ga * jnp.dot(sa, wco_ref[...], preferred_element_type=F32)

    heads = range(n_heads)
    for ci in range(n_chunks):
        rows = slice(ci * RET_CHUNK, (ci + 1) * RET_CHUNK)
        qs = [slice(hh * dk, (hh + 1) * dk) for hh in heads]
        vs = [slice(hh * dv, (hh + 1) * dv) for hh in heads]
        sc = [lax.dot_general(q_ref[rows, qs[hh]], k_ref[rows, qs[hh]], (((1,), (1,)), ((), ())),
                              preferred_element_type=F32) for hh in heads]
        lhs = [jnp.concatenate([(sc[hh] * decay_ref[hh]).astype(BF16), qx_ref[rows, qs[hh]]], axis=1)
               for hh in heads]
        st = [state_ref[hh] for hh in heads]
        vh = [v_ref[rows, vs[hh]] for hh in heads]
        o = [jnp.dot(lhs[hh], jnp.concatenate([vh[hh], st[hh].astype(BF16)], axis=0),
                     preferred_element_type=F32) for hh in heads]
        kv = [lax.dot_general(kz_ref[rows, qs[hh]], vh[hh], (((0,), (0,)), ((), ())),
                              preferred_element_type=F32) for hh in heads]
        for hh in heads:
            state_ref[hh] = st[hh] * chunk_decay[hh] + kv[hh]
        for hh in heads:
            r = (_ln(o[hh]) * gng_ref[:, vs[hh]] + gnb_ref[:, vs[hh]]) * sg_ref[rows, vs[hh]]
            ract_ref[rows, vs[hh]] = r.astype(BF16)

    yb = jnp.dot(ract_ref[...], wro_ref[...], preferred_element_type=F32)
    o_ref[...] = (ua + gb * yb).astype(BF16)


def _token_branches(x2, pos, mod, w_in, b_in, cw, cb, clg, clb, w_co, gn_g, gn_b, w_ro, tile):
    s_len, d = x2.shape
    dk = d // N_HEADS
    dv = w_ro.shape[0] // N_HEADS
    log_gamma = np.log(1.0 - 2.0 ** (-5.0 - np.arange(N_HEADS, dtype=np.float64)))
    idx = np.arange(RET_CHUNK, dtype=np.float64)
    rel = idx[:, None] - idx[None, :]
    decay = np.where(rel[None] >= 0, np.exp(log_gamma[:, None, None] * np.maximum(rel, 0.0)[None]), 0.0)
    xi = np.repeat(np.exp(log_gamma[None, :] * (idx[:, None] + 1.0)), dk, axis=1)
    zeta = np.repeat(np.exp(log_gamma[None, :] * (RET_CHUNK - 1.0 - idx[:, None])), dk, axis=1)
    chunk_decay = tuple(float(v) for v in np.exp(log_gamma * RET_CHUNK))
    half = dk // 2
    inv_freq = ROPE_BASE ** (-jnp.arange(half, dtype=F32) / half)
    inv_freq = jnp.concatenate([inv_freq, inv_freq]).reshape(1, dk)

    kern = functools.partial(_branches_kernel, tile=tile, d_model=d, dk=dk, dv=dv, chunk_decay=chunk_decay)
    return pl.pallas_call(
        kern,
        grid=(s_len // tile,),
        in_specs=[
            _rows(tile, d),
            _rows(tile, 1),
            _resident((1, dk)),
            _resident((N_MOD, d)),
            _HBM,
            _resident(b_in.shape),
            _resident(cw.shape),
            _resident((1, d)),
            _resident((1, d)),
            _resident((1, d)),
            _HBM,
            _resident((1, N_HEADS * dv)),
            _resident((1, N_HEADS * dv)),
            _HBM,
            _resident((N_HEADS, RET_CHUNK, RET_CHUNK)),
            _resident((RET_CHUNK, d)),
            _resident((RET_CHUNK, d)),
        ],
        out_specs=_rows(tile, d),
        out_shape=jax.ShapeDtypeStruct((s_len, d), BF16),
        scratch_shapes=[
            pltpu.VMEM((N_HEADS, dk, dv), F32),
            pltpu.VMEM((d // V7X_LANES, CONV_HIST + tile, V7X_LANES), F32),
            pltpu.VMEM((tile, d), BF16),
            pltpu.VMEM((tile, d), BF16),
            pltpu.VMEM((tile, d), BF16),
            pltpu.VMEM((tile, d), BF16),
            pltpu.VMEM((tile, N_HEADS * dv), BF16),
            pltpu.VMEM((tile, N_HEADS * dv), F32),
            pltpu.VMEM((tile, N_HEADS * dv), BF16),
            pltpu.VMEM(w_in.shape, BF16),
            pltpu.VMEM(w_co.shape, BF16),
            pltpu.VMEM(w_ro.shape, BF16),
            *_stage_scratch(w_in.shape[1], TOKEN_STAGE_CHUNK_BYTES, slots=WIDE_STAGE_SLOTS),
            *_stage_scratch(d, 2 * TOKEN_STAGE_CHUNK_BYTES),
        ],
        compiler_params=_params(),
        name="token_branches",
    )(x2, pos, inv_freq, mod, w_in, b_in, cw, cb, clg, clb, w_co, gn_g, gn_b, w_ro,
      jnp.asarray(decay, F32), jnp.asarray(xi, F32), jnp.asarray(zeta, F32))


def _out_ffn_kernel(x_ref, m_ref, mod_ref, wout_hbm, l1g_ref, l1b_ref, wup_hbm, fw_ref, fb_ref, wdn_hbm,
                    l2g_ref, l2b_ref, o_ref, ubuf_ref, act_ref,
                    wout_ref, wup_ref, wdn_ref, stage_up, sem_up, stage_d, sem_d, *, tile, d_ff):
    n_lane_chunks = 2 * d_ff // V7X_LANES

    @pl.when(pl.program_id(0) == 0)
    def _():
        ubuf_ref[:, 0:FFN_HIST, :] = jnp.zeros((n_lane_chunks, FFN_HIST, V7X_LANES), F32)
        _load_weight_bf16(wout_hbm, wout_ref, stage_d, sem_d)
        _load_weight_bf16(wup_hbm, wup_ref, stage_up, sem_up)
        _load_weight_bf16(wdn_hbm, wdn_ref, stage_d, sem_d)

    t = jnp.dot(m_ref[...], wout_ref[...], preferred_element_type=F32)
    z1 = DEEPNORM_ALPHA * x_ref[...] + mod_ref[2:3, :] * t
    x1 = _ln(z1) * l1g_ref[...] + l1b_ref[...]

    h = _ln(x1) * (1.0 + mod_ref[4:5, :]) + mod_ref[3:4, :]
    hb = h.astype(BF16)

    def conv_cols(base):
        u = jnp.dot(hb, wup_ref[:, base:base + FFN_COLS], preferred_element_type=F32)
        outs = []
        for jj in range(FFN_COLS // V7X_LANES):
            j = base // V7X_LANES + jj
            ls = slice(j * V7X_LANES, (j + 1) * V7X_LANES)
            uj = u[:, jj * V7X_LANES:(jj + 1) * V7X_LANES]
            ubuf_ref[j, FFN_HIST:FFN_HIST + tile, :] = uj
            y = fb_ref[:, ls] + fw_ref[2:3, ls] * uj
            for k in range(FFN_CONV_KERNEL - 1):
                r0 = FFN_HIST - (FFN_CONV_KERNEL - 1) + k
                y = y + fw_ref[k:k + 1, ls] * ubuf_ref[j, r0:r0 + tile, :]
            ubuf_ref[j, 0:FFN_HIST, :] = ubuf_ref[j, tile:tile + FFN_HIST, :]
            outs.append(y)
        return jnp.concatenate(outs, axis=1)

    for gi in range(d_ff // FFN_COLS):
        val = conv_cols(gi * FFN_COLS)
        gate = conv_cols(d_ff + gi * FFN_COLS)
        act_ref[:, gi * FFN_COLS:(gi + 1) * FFN_COLS] = (val * (gate * _sigmoid(gate))).astype(BF16)

    f = jnp.dot(act_ref[...], wdn_ref[...], preferred_element_type=F32)
    z2 = DEEPNORM_ALPHA * x1 + mod_ref[5:6, :] * f
    o_ref[...] = _ln(z2) * l2g_ref[...] + l2b_ref[...]


def _out_ffn(x2, m, mod, w_out, l1g, l1b, w_up, fw, fb, w_dn, l2g, l2b, tile):
    s_len, d = x2.shape
    d_ff = w_dn.shape[0]
    kern = functools.partial(_out_ffn_kernel, tile=tile, d_ff=d_ff)
    return pl.pallas_call(
        kern,
        grid=(s_len // tile,),
        in_specs=[
            _rows(tile, d),
            _rows(tile, d),
            _resident((N_MOD, d)),
            _HBM,
            _resident((1, d)),
            _resident((1, d)),
            _HBM,
            _resident(fw.shape),
            _resident((1, 2 * d_ff)),
            _HBM,
            _resident((1, d)),
            _resident((1, d)),
        ],
        out_specs=_rows(tile, d),
        out_shape=jax.ShapeDtypeStruct((s_len, d), F32),
        scratch_shapes=[
            pltpu.VMEM((2 * d_ff // V7X_LANES, FFN_HIST + tile, V7X_LANES), F32),
            pltpu.VMEM((tile, d_ff), BF16),
            pltpu.VMEM(w_out.shape, BF16),
            pltpu.VMEM(w_up.shape, BF16),
            pltpu.VMEM(w_dn.shape, BF16),
            *_stage_scratch(w_up.shape[1], FFN_STAGE_CHUNK_BYTES, slots=WIDE_STAGE_SLOTS),
            *_stage_scratch(d, FFN_STAGE_CHUNK_BYTES // 2),
        ],
        compiler_params=_params(),
        name="out_ffn",
    )(x2, m, mod, w_out, l1g, l1b, w_up, fw, fb, w_dn, l2g, l2b)


def kernel(x, c, positions, w_ada, b_ada, w_in, b_in, conv_dw_w, conv_dw_b, conv_ln_g, conv_ln_b,
           w_conv_out, ret_gn_g, ret_gn_b, w_ret_out, w_out, ln1_g, ln1_b, w_up, ffn_dw_w, ffn_dw_b,
           w_down, ln2_g, ln2_b):
    bsz, s_len, d = x.shape
    assert bsz == 1 and s_len % SEQ_TILE == 0 and d % V7X_LANES == 0
    depth = w_ada.shape[0]

    row = lambda v: v.reshape(1, -1)
    x2 = x.reshape(s_len, d)
    pos = positions.reshape(s_len, 1).astype(F32)
    for l in range(depth):
        mod = _adaln_mod(c, w_ada[l], b_ada[l])
        m = _token_branches(x2, pos, mod, w_in[l], row(b_in[l]), conv_dw_w[l], row(conv_dw_b[l]),
                            row(conv_ln_g[l]), row(conv_ln_b[l]), w_conv_out[l],
                            row(ret_gn_g[l]), row(ret_gn_b[l]), w_ret_out[l], SEQ_TILE)
        x2 = _out_ffn(x2, m, mod, w_out[l], row(ln1_g[l]), row(ln1_b[l]), w_up[l],
                      ffn_dw_w[l], row(ffn_dw_b[l]), w_down[l], row(ln2_g[l]), row(ln2_b[l]),
                      SEQ_TILE)
    return x2.reshape(bsz, s_len, d)
```

```python
import functools

import jax
import jax.numpy as jnp
import numpy as np
from jax import lax
from jax.experimental import pallas as pl
from jax.experimental.pallas import tpu as pltpu

F32 = jnp.float32
BF16 = jnp.bfloat16

DEPTH = 1
N_HEADS = 8
RET_CHUNK = 128
CONV_KERNEL = 31
FFN_CONV_KERNEL = 3
ROPE_BASE = 10000.0
LN_EPS = 1e-5
DEEPNORM_ALPHA = (2.0 * DEPTH) ** 0.25
N_MOD = 6

V7X_LANES = 128
V7X_SUBLANES = 8

SEQ_TILE = 512
CONV_HIST = 32
FFN_HIST = V7X_SUBLANES
CONV_ROWS = 128
FFN_COLS = 256
LN_ROWS = 128
VMEM_LIMIT_BYTES = 63 * 1024 * 1024
TOKEN_STAGE_CHUNK_BYTES = 256 * 1024
FFN_STAGE_CHUNK_BYTES = 1024 * 1024
STAGE_SLOTS = 4
WIDE_STAGE_SLOTS = 6


def _ln(x):
    mu = jnp.mean(x, axis=-1, keepdims=True)
    xc = x - mu
    var = jnp.mean(xc * xc, axis=-1, keepdims=True)
    return xc * lax.rsqrt(var + LN_EPS)


def _sigmoid(x):
    return 0.5 * jnp.tanh(0.5 * x) + 0.5


def _resident(shape):
    nd = len(shape)
    return pl.BlockSpec(shape, lambda i: (0,) * nd, pipeline_mode=pl.Buffered(1))


def _rows(tile, width):
    return pl.BlockSpec((tile, width), lambda i: (i, 0))


def _params():
    return pltpu.CompilerParams(dimension_semantics=("arbitrary",), vmem_limit_bytes=VMEM_LIMIT_BYTES)


_HBM = pl.BlockSpec(memory_space=pl.ANY)


def _stage_scratch(width, chunk_bytes, slots=STAGE_SLOTS):
    rows = max(chunk_bytes // (width * 4), 2 * V7X_SUBLANES)
    rows -= rows % (2 * V7X_SUBLANES)
    return [pltpu.VMEM((slots, rows, width), F32), pltpu.SemaphoreType.DMA((slots,))]


def _load_weight_bf16(w_hbm, w_vmem, stage, sem):
    n_slots, rows = stage.shape[0], stage.shape[1]
    n_chunks = w_hbm.shape[0] // rows
    assert n_chunks * rows == w_hbm.shape[0] and stage.shape[2] == w_hbm.shape[1] and n_chunks >= n_slots

    def copy(c, slot):
        r0 = pl.multiple_of(c * rows, rows)
        return pltpu.make_async_copy(w_hbm.at[pl.ds(r0, rows), :], stage.at[slot], sem.at[slot])

    for c in range(n_slots - 1):
        copy(c, c).start()

    def body(c, carry):
        slot = c % n_slots
        ahead = c + n_slots - 1

        @pl.when(ahead < n_chunks)
        def _():
            copy(ahead, ahead % n_slots).start()

        copy(c, slot).wait()
        r0 = pl.multiple_of(c * rows, rows)
        w_vmem[pl.ds(r0, rows), :] = stage[slot].astype(BF16)
        return carry

    lax.fori_loop(0, n_chunks, body, 0)


def _mod_kernel(c_ref, w_ref, b_ref, o_ref):
    c = c_ref[...]
    s = c * _sigmoid(c)
    o_ref[...] = jnp.dot(s, w_ref[...], preferred_element_type=F32) + b_ref[...]


def _adaln_mod(c, w_ada, b_ada):
    d = c.shape[-1]
    c8 = jnp.broadcast_to(c, (V7X_SUBLANES, d))
    out = pl.pallas_call(
        _mod_kernel,
        grid=(N_MOD,),
        in_specs=[
            pl.BlockSpec((V7X_SUBLANES, d), lambda i: (0, 0)),
            pl.BlockSpec((d, d), lambda i: (0, i)),
            pl.BlockSpec((1, d), lambda i: (0, i)),
        ],
        out_specs=pl.BlockSpec((V7X_SUBLANES, d), lambda i: (0, i)),
        out_shape=jax.ShapeDtypeStruct((V7X_SUBLANES, N_MOD * d), F32),
        compiler_params=pltpu.CompilerParams(dimension_semantics=("arbitrary",)),
        name="adaln_mod",
    )(c8, w_ada, b_ada.reshape(1, -1))
    return out[0].reshape(N_MOD, d)


def _branches_kernel(x_ref, pos_ref, invf_ref, mod_ref, w_hbm, b_ref, cw_ref, cb_ref, clg_ref, clb_ref,
                     wco_hbm, gng_ref, gnb_ref, wro_hbm, decay_ref, xi_ref, zeta_ref, o_ref,
                     state_ref, abuf_ref, q_ref, qx_ref, k_ref, kz_ref, v_ref, sg_ref, ract_ref,
                     w_ref, wco_ref, wro_ref, stage_in, sem_in, stage_d, sem_d,
                     *, tile, d_model, dk, dv, chunk_decay):
    n_heads = d_model // dk
    v_w = n_heads * dv
    n_lane_chunks = d_model // V7X_LANES
    n_chunks = tile // RET_CHUNK
    o_q = 0
    o_k = o_q + d_model
    o_v = o_k + d_model
    o_g = o_v + v_w
    o_cv = o_g + v_w
    o_cg = o_cv + d_model
    o_ga = o_cg + d_model
    o_gb = o_ga + d_model

    @pl.when(pl.program_id(0) == 0)
    def _():
        state_ref[...] = jnp.zeros_like(state_ref)
        abuf_ref[:, 0:CONV_HIST, :] = jnp.zeros((n_lane_chunks, CONV_HIST, V7X_LANES), F32)
        _load_weight_bf16(w_hbm, w_ref, stage_in, sem_in)
        _load_weight_bf16(wco_hbm, wco_ref, stage_d, sem_d)
        _load_weight_bf16(wro_hbm, wro_ref, stage_d, sem_d)

    x = x_ref[...]
    h = _ln(x) * (1.0 + mod_ref[1:2, :]) + mod_ref[0:1, :]
    hb = h.astype(BF16)

    def proj(lo, width):
        return jnp.dot(hb, w_ref[:, lo:lo + width], preferred_element_type=F32) + b_ref[:, lo:lo + width]

    def anchor(v):
        bits = lax.bitcast_convert_type(v[0:1, 0:V7X_LANES].astype(F32), jnp.uint32)
        bits = lax.shift_right_logical(lax.shift_right_logical(bits, jnp.uint32(16)), jnp.uint32(16))
        return lax.bitcast_convert_type(bits, F32)

    a = proj(o_cv, d_model) * _sigmoid(proj(o_cg, d_model))
    for j in range(n_lane_chunks):
        abuf_ref[j, CONV_HIST:CONV_HIST + tile, :] = a[:, j * V7X_LANES:(j + 1) * V7X_LANES]
    first = CONV_HIST - (CONV_KERNEL - 1)

    def conv_chunk(j, zero):
        ls = slice(j * V7X_LANES, (j + 1) * V7X_LANES)
        bias = cb_ref[:, ls] if zero is None else cb_ref[:, ls] + zero
        blocks = []
        for rb in range(tile // CONV_ROWS):
            acc = jnp.broadcast_to(bias, (CONV_ROWS, V7X_LANES))
            for k in range(CONV_KERNEL):
                r0 = rb * CONV_ROWS + first + k
                acc = acc + cw_ref[k:k + 1, ls] * abuf_ref[j, r0:r0 + CONV_ROWS, :]
            blocks.append(acc)
        abuf_ref[j, 0:CONV_HIST, :] = abuf_ref[j, tile:tile + CONV_HIST, :]
        return jnp.concatenate(blocks, axis=0)

    half_t = tile // 2
    lane = lax.broadcasted_iota(jnp.int32, (half_t, dk), 1)
    low = lane < dk // 2
    ang = jnp.where(low, pos_ref[0:half_t, :], pos_ref[half_t:tile, :]) * invf_ref[...]
    cs_p, sn_p = jnp.cos(ang), jnp.sin(ang)
    cs_r, sn_r = pltpu.roll(cs_p, dk // 2, 1), pltpu.roll(sn_p, dk // 2, 1)
    cs = jnp.concatenate([jnp.where(low, cs_p, cs_r), jnp.where(low, cs_r, cs_p)], axis=0)
    sn = jnp.concatenate([jnp.where(low, -sn_p, sn_r), jnp.where(low, -sn_r, sn_p)], axis=0)
    k_scale = float(dk) ** -0.5

    def rope_heads(y, cos_t, sin_t):
        for hh in range(n_heads):
            yh = y[:, hh * dk:(hh + 1) * dk]
            yield hh, yh * cos_t + pltpu.roll(yh, dk // 2, 1) * sin_t

    def tiled(tab_ref, hh):
        return jnp.concatenate([tab_ref[:, hh * dk:(hh + 1) * dk]] * n_chunks, axis=0)

    half_v = v_w // 2
    vb0 = proj(o_v, half_v).astype(BF16)
    v_ref[:, 0:half_v] = vb0
    vb1 = proj(o_v + half_v, half_v).astype(BF16)
    v_ref[:, half_v:v_w] = vb1
    pins = {3: anchor(vb0), 6: anchor(vb1)}
    cols = [conv_chunk(j, pins.get(j)) for j in range(n_lane_chunks)]
    for hh, r in rope_heads(proj(o_q, d_model), cs, sn):
        q_ref[:, hh * dk:(hh + 1) * dk] = r.astype(BF16)
        qx_ref[:, hh * dk:(hh + 1) * dk] = (r * tiled(xi_ref, hh)).astype(BF16)
    for hh, r in rope_heads(proj(o_k, d_model), cs * k_scale, sn * k_scale):
        k_ref[:, hh * dk:(hh + 1) * dk] = r.astype(BF16)
        kz_ref[:, hh * dk:(hh + 1) * dk] = (r * tiled(zeta_ref, hh)).astype(BF16)
    g = proj(o_g, half_v)
    sg_ref[:, 0:half_v] = g * _sigmoid(g)
    g = proj(o_g + half_v, half_v)
    sg_ref[:, half_v:v_w] = g * _sigmoid(g)
    ga = _sigmoid(proj(o_ga, d_model))
    gb = _sigmoid(proj(o_gb, d_model))

    an = _ln(jnp.concatenate(cols, axis=1)) * clg_ref[...] + clb_ref[...]
    sa = (an * _sigmoid(an)).astype(BF16)
    ua = ga * jnp.dot(sa, wco_ref[...], preferred_element_type=F32)

    heads = range(n_heads)
    for ci in range(n_chunks):
        rows = slice(ci * RET_CHUNK, (ci + 1) * RET_CHUNK)
        qs = [slice(hh * dk, (hh + 1) * dk) for hh in heads]
        vs = [slice(hh * dv, (hh + 1) * dv) for hh in heads]
        sc = [lax.dot_general(q_ref[rows, qs[hh]], k_ref[rows, qs[hh]], (((1,), (1,)), ((), ())),
                              preferred_element_type=F32) for hh in heads]
        lhs = [jnp.concatenate([(sc[hh] * decay_ref[hh]).astype(BF16), qx_ref[rows, qs[hh]]], axis=1)
               for hh in heads]
        st = [state_ref[hh] for hh in heads]
        vh = [v_ref[rows, vs[hh]] for hh in heads]
        o = [jnp.dot(lhs[hh], jnp.concatenate([vh[hh], st[hh].astype(BF16)], axis=0),
                     preferred_element_type=F32) for hh in heads]
        kv = [lax.dot_general(kz_ref[rows, qs[hh]], vh[hh], (((0,), (0,)), ((), ())),
                              preferred_element_type=F32) for hh in heads]
        for hh in heads:
            state_ref[hh] = st[hh] * chunk_decay[hh] + kv[hh]
        for hh in heads:
            r = (_ln(o[hh]) * gng_ref[:, vs[hh]] + gnb_ref[:, vs[hh]]) * sg_ref[rows, vs[hh]]
            ract_ref[rows, vs[hh]] = r.astype(BF16)

    yb = jnp.dot(ract_ref[...], wro_ref[...], preferred_element_type=F32)
    o_ref[...] = (ua + gb * yb).astype(BF16)


def _token_branches(x2, pos, mod, w_in, b_in, cw, cb, clg, clb, w_co, gn_g, gn_b, w_ro, tile):
    s_len, d = x2.shape
    dk = d // N_HEADS
    dv = w_ro.shape[0] // N_HEADS
    log_gamma = np.log(1.0 - 2.0 ** (-5.0 - np.arange(N_HEADS, dtype=np.float64)))
    idx = np.arange(RET_CHUNK, dtype=np.float64)
    rel = idx[:, None] - idx[None, :]
    decay = np.where(rel[None] >= 0, np.exp(log_gamma[:, None, None] * np.maximum(rel, 0.0)[None]), 0.0)
    xi = np.repeat(np.exp(log_gamma[None, :] * (idx[:, None] + 1.0)), dk, axis=1)
    zeta = np.repeat(np.exp(log_gamma[None, :] * (RET_CHUNK - 1.0 - idx[:, None])), dk, axis=1)
    chunk_decay = tuple(float(v) for v in np.exp(log_gamma * RET_CHUNK))
    half = dk // 2
    inv_freq = ROPE_BASE ** (-jnp.arange(half, dtype=F32) / half)
    inv_freq = jnp.concatenate([inv_freq, inv_freq]).reshape(1, dk)

    kern = functools.partial(_branches_kernel, tile=tile, d_model=d, dk=dk, dv=dv, chunk_decay=chunk_decay)
    return pl.pallas_call(
        kern,
        grid=(s_len // tile,),
        in_specs=[
            _rows(tile, d),
            _rows(tile, 1),
            _resident((1, dk)),
            _resident((N_MOD, d)),
            _HBM,
            _resident(b_in.shape),
            _resident(cw.shape),
            _resident((1, d)),
            _resident((1, d)),
            _resident((1, d)),
            _HBM,
            _resident((1, N_HEADS * dv)),
            _resident((1, N_HEADS * dv)),
            _HBM,
            _resident((N_HEADS, RET_CHUNK, RET_CHUNK)),
            _resident((RET_CHUNK, d)),
            _resident((RET_CHUNK, d)),
        ],
        out_specs=_rows(tile, d),
        out_shape=jax.ShapeDtypeStruct((s_len, d), BF16),
        scratch_shapes=[
            pltpu.VMEM((N_HEADS, dk, dv), F32),
            pltpu.VMEM((d // V7X_LANES, CONV_HIST + tile, V7X_LANES), F32),
            pltpu.VMEM((tile, d), BF16),
            pltpu.VMEM((tile, d), BF16),
            pltpu.VMEM((tile, d), BF16),
            pltpu.VMEM((tile, d), BF16),
            pltpu.VMEM((tile, N_HEADS * dv), BF16),
            pltpu.VMEM((tile, N_HEADS * dv), F32),
            pltpu.VMEM((tile, N_HEADS * dv), BF16),
            pltpu.VMEM(w_in.shape, BF16),
            pltpu.VMEM(w_co.shape, BF16),
            pltpu.VMEM(w_ro.shape, BF16),
            *_stage_scratch(w_in.shape[1], TOKEN_STAGE_CHUNK_BYTES, slots=WIDE_STAGE_SLOTS),
            *_stage_scratch(d, 2 * TOKEN_STAGE_CHUNK_BYTES),
        ],
        compiler_params=_params(),
        name="token_branches",
    )(x2, pos, inv_freq, mod, w_in, b_in, cw, cb, clg, clb, w_co, gn_g, gn_b, w_ro,
      jnp.asarray(decay, F32), jnp.asarray(xi, F32), jnp.asarray(zeta, F32))


def _out_ffn_kernel(x_ref, m_ref, mod_ref, wout_hbm, l1g_ref, l1b_ref, wup_hbm, fw_ref, fb_ref, wdn_hbm,
                    l2g_ref, l2b_ref, o_ref, ubuf_ref, act_ref,
                    wout_ref, wup_ref, wdn_ref, stage_up, sem_up, stage_d, sem_d, *, tile, d_ff):
    n_lane_chunks = 2 * d_ff // V7X_LANES

    @pl.when(pl.program_id(0) == 0)
    def _():
        ubuf_ref[:, 0:FFN_HIST, :] = jnp.zeros((n_lane_chunks, FFN_HIST, V7X_LANES), F32)
        _load_weight_bf16(wout_hbm, wout_ref, stage_d, sem_d)
        _load_weight_bf16(wup_hbm, wup_ref, stage_up, sem_up)
        _load_weight_bf16(wdn_hbm, wdn_ref, stage_d, sem_d)

    t = jnp.dot(m_ref[...], wout_ref[...], preferred_element_type=F32)
    x1_blocks, hb_blocks = [], []
    for rb in range(tile // LN_ROWS):
        rows = slice(rb * LN_ROWS, (rb + 1) * LN_ROWS)
        z1 = DEEPNORM_ALPHA * x_ref[rows, :] + mod_ref[2:3, :] * t[rows]
        x1_b = _ln(z1) * l1g_ref[...] + l1b_ref[...]
        x1_blocks.append(x1_b)
        hb_blocks.append((_ln(x1_b) * (1.0 + mod_ref[4:5, :]) + mod_ref[3:4, :]).astype(BF16))
    hb = jnp.concatenate(hb_blocks, axis=0)

    def conv_cols(base):
        u = jnp.dot(hb, wup_ref[:, base:base + FFN_COLS], preferred_element_type=F32)
        outs = []
        for jj in range(FFN_COLS // V7X_LANES):
            j = base // V7X_LANES + jj
            ls = slice(j * V7X_LANES, (j + 1) * V7X_LANES)
            uj = u[:, jj * V7X_LANES:(jj + 1) * V7X_LANES]
            ubuf_ref[j, FFN_HIST:FFN_HIST + tile, :] = uj
            y = fb_ref[:, ls] + fw_ref[2:3, ls] * uj
            for k in range(FFN_CONV_KERNEL - 1):
                r0 = FFN_HIST - (FFN_CONV_KERNEL - 1) + k
                y = y + fw_ref[k:k + 1, ls] * ubuf_ref[j, r0:r0 + tile, :]
            ubuf_ref[j, 0:FFN_HIST, :] = ubuf_ref[j, tile:tile + FFN_HIST, :]
            outs.append(y)
        return jnp.concatenate(outs, axis=1)

    for gi in range(d_ff // FFN_COLS):
        val = conv_cols(gi * FFN_COLS)
        gate = conv_cols(d_ff + gi * FFN_COLS)
        act_ref[:, gi * FFN_COLS:(gi + 1) * FFN_COLS] = (val * (gate * _sigmoid(gate))).astype(BF16)

    f = jnp.dot(act_ref[...], wdn_ref[...], preferred_element_type=F32)
    for rb in range(tile // LN_ROWS):
        rows = slice(rb * LN_ROWS, (rb + 1) * LN_ROWS)
        z2 = DEEPNORM_ALPHA * x1_blocks[rb] + mod_ref[5:6, :] * f[rows]
        o_ref[rows, :] = _ln(z2) * l2g_ref[...] + l2b_ref[...]


def _out_ffn(x2, m, mod, w_out, l1g, l1b, w_up, fw, fb, w_dn, l2g, l2b, tile):
    s_len, d = x2.shape
    d_ff = w_dn.shape[0]
    kern = functools.partial(_out_ffn_kernel, tile=tile, d_ff=d_ff)
    return pl.pallas_call(
        kern,
        grid=(s_len // tile,),
        in_specs=[
            _rows(tile, d),
            _rows(tile, d),
            _resident((N_MOD, d)),
            _HBM,
            _resident((1, d)),
            _resident((1, d)),
            _HBM,
            _resident(fw.shape),
            _resident((1, 2 * d_ff)),
            _HBM,
            _resident((1, d)),
            _resident((1, d)),
        ],
        out_specs=_rows(tile, d),
        out_shape=jax.ShapeDtypeStruct((s_len, d), F32),
        scratch_shapes=[
            pltpu.VMEM((2 * d_ff // V7X_LANES, FFN_HIST + tile, V7X_LANES), F32),
            pltpu.VMEM((tile, d_ff), BF16),
            pltpu.VMEM(w_out.shape, BF16),
            pltpu.VMEM(w_up.shape, BF16),
            pltpu.VMEM(w_dn.shape, BF16),
            *_stage_scratch(w_up.shape[1], FFN_STAGE_CHUNK_BYTES, slots=WIDE_STAGE_SLOTS),
            *_stage_scratch(d, FFN_STAGE_CHUNK_BYTES),
        ],
        compiler_params=_params(),
        name="out_ffn",
    )(x2, m, mod, w_out, l1g, l1b, w_up, fw, fb, w_dn, l2g, l2b)


def kernel(x, c, positions, w_ada, b_ada, w_in, b_in, conv_dw_w, conv_dw_b, conv_ln_g, conv_ln_b,
           w_conv_out, ret_gn_g, ret_gn_b, w_ret_out, w_out, ln1_g, ln1_b, w_up, ffn_dw_w, ffn_dw_b,
           w_down, ln2_g, ln2_b):
    bsz, s_len, d = x.shape
    assert bsz == 1 and s_len % SEQ_TILE == 0 and d % V7X_LANES == 0
    depth = w_ada.shape[0]

    row = lambda v: v.reshape(1, -1)
    x2 = x.reshape(s_len, d)
    pos = positions.reshape(s_len, 1).astype(F32)
    for l in range(depth):
        mod = _adaln_mod(c, w_ada[l], b_ada[l])
        m = _token_branches(x2, pos, mod, w_in[l], row(b_in[l]), conv_dw_w[l], row(conv_dw_b[l]),
                            row(conv_ln_g[l]), row(conv_ln_b[l]), w_conv_out[l],
                            row(ret_gn_g[l]), row(ret_gn_b[l]), w_ret_out[l], SEQ_TILE)
        x2 = _out_ffn(x2, m, mod, w_out[l], row(ln1_g[l]), row(ln1_b[l]), w_up[l],
                      ffn_dw_w[l], row(ffn_dw_b[l]), w_down[l], row(ln2_g[l]), row(ln2_b[l]),
                      SEQ_TILE)
    return x2.reshape(bsz, s_len, d)
```
